```python
import math
import jax, jax.numpy as jnp
from jax import lax
import numpy as np

D_MODEL = 1024
BATCH = 2
SEQ = 8192
DEPTH = 2

N_MIXERS = 2
N_ATTN_LAYERS = (DEPTH + 1) // 2
N_POOL_LAYERS = DEPTH // 2

HEAD_DIM = 64
N_Q_HEADS = D_MODEL // HEAD_DIM
N_KV_HEADS = 4
GROUP = N_Q_HEADS // N_KV_HEADS
QKV_DIM = (N_Q_HEADS + 2 * N_KV_HEADS) * HEAD_DIM
WINDOW = 128
BLOCK = 128
ROPE_DIM = HEAD_DIM // 4
ROPE_THETA = 500000.0

POOL_WINDOWS = (2, 4, 8, 16)
N_POOL_GROUPS = len(POOL_WINDOWS)
POOL_GROUP_DIM = D_MODEL // N_POOL_GROUPS

D_FF = 2816
CONV_WIDTH = 3

NORM_EPS = 1e-6
NEG_INF = -1e30

kernel_name = "hybrid_swa_sink_pool_convffn_adaln"


def rmsnorm(x, gain):
    xf = x.astype(jnp.float32)
    y = xf * lax.rsqrt(jnp.mean(xf * xf, axis=-1, keepdims=True) + NORM_EPS)
    return (y * gain.astype(jnp.float32)).astype(x.dtype)


def rope_tables(seq_len):
    half = ROPE_DIM // 2
    inv_freq = ROPE_THETA ** (-jnp.arange(0, half, dtype=jnp.float32) * 2.0 / ROPE_DIM)
    pos = jnp.arange(seq_len, dtype=jnp.float32)
    ang = pos[:, None] * inv_freq[None, :]
    return jnp.cos(ang), jnp.sin(ang)


def apply_partial_rope(x, cos, sin):
    half = ROPE_DIM // 2
    xr = x[..., :ROPE_DIM].astype(jnp.float32)
    x1, x2 = xr[..., :half], xr[..., half:]
    rot = jnp.concatenate([x1 * cos - x2 * sin, x2 * cos + x1 * sin], axis=-1)
    return jnp.concatenate([rot.astype(x.dtype), x[..., ROPE_DIM:]], axis=-1)


def swa_sink_attention(h, w_qkv, q_gain, k_gain, sinks, w_o):
    B, S, _ = h.shape
    nb = S // BLOCK
    qkv = h @ w_qkv
    q = qkv[..., :N_Q_HEADS * HEAD_DIM].reshape(B, S, N_KV_HEADS, GROUP, HEAD_DIM)
    k = qkv[..., N_Q_HEADS * HEAD_DIM:(N_Q_HEADS + N_KV_HEADS) * HEAD_DIM].reshape(B, S, N_KV_HEADS, HEAD_DIM)
    v = qkv[..., (N_Q_HEADS + N_KV_HEADS) * HEAD_DIM:].reshape(B, S, N_KV_HEADS, HEAD_DIM)
    q = rmsnorm(q, q_gain)
    k = rmsnorm(k, k_gain)
    cos, sin = rope_tables(S)
    q = apply_partial_rope(q, cos[None, :, None, None, :], sin[None, :, None, None, :])
    k = apply_partial_rope(k, cos[None, :, None, :], sin[None, :, None, :])

    qb = q.reshape(B, nb, BLOCK, N_KV_HEADS, GROUP, HEAD_DIM)
    kb = k.reshape(B, nb, BLOCK, N_KV_HEADS, HEAD_DIM)
    vb = v.reshape(B, nb, BLOCK, N_KV_HEADS, HEAD_DIM)
    k_prev = jnp.concatenate([jnp.zeros_like(kb[:, :1]), kb[:, :-1]], axis=1)
    v_prev = jnp.concatenate([jnp.zeros_like(vb[:, :1]), vb[:, :-1]], axis=1)
    k_band = jnp.concatenate([k_prev, kb], axis=2)
    v_band = jnp.concatenate([v_prev, vb], axis=2)

    scale = 1.0 / math.sqrt(HEAD_DIM)
    scores = jnp.einsum('bnqhgd,bnkhd->bnhgqk', qb, k_band).astype(jnp.float32) * scale

    qi = jnp.arange(BLOCK)[:, None]
    kj = jnp.arange(2 * BLOCK)[None, :]
    dist = BLOCK + qi - kj
    in_window = (dist >= 0) & (dist < WINDOW)
    blk = jnp.arange(nb)[:, None, None]
    key_pos = blk * BLOCK + kj[None] - BLOCK
    valid = in_window[None] & (key_pos >= 0)
    scores = jnp.where(valid[None, :, None, None], scores, NEG_INF)

    sink = sinks.astype(jnp.float32).reshape(1, 1, N_KV_HEADS, GROUP, 1, 1)
    m = jnp.maximum(jnp.max(scores, axis=-1, keepdims=True), sink)
    p = jnp.exp(scores - m)
    denom = jnp.sum(p, axis=-1, keepdims=True) + jnp.exp(sink - m)
    probs = (p / denom).astype(v.dtype)

    out = jnp.einsum('bnhgqk,bnkhd->bnqhgd', probs, v_band)
    return out.reshape(B, S, N_Q_HEADS * HEAD_DIM) @ w_o


def multiscale_pool_mixer(h, pool_w, pool_scale):
    B, S, D = h.shape
    hf = h.astype(jnp.float32)
    cs = jnp.concatenate([jnp.zeros((B, 1, D), jnp.float32), jnp.cumsum(hf, axis=1)], axis=1)
    t1 = jnp.arange(1, S + 1, dtype=jnp.float32)[None, :, None]
    diffs = []
    for g, w in enumerate(POOL_WINDOWS):
        sl = slice(g * POOL_GROUP_DIM, (g + 1) * POOL_GROUP_DIM)
        csg = cs[:, :, sl]
        win_sum = jnp.concatenate([csg[:, 1:w], csg[:, w:] - csg[:, :S + 1 - w]], axis=1)
        count = jnp.minimum(t1, float(w))
        diffs.append(win_sum / count - hf[:, :, sl])
    d = jnp.stack(diffs, axis=2)
    y = jnp.einsum('bsgc,gce->bsge', d.astype(h.dtype), pool_w).reshape(B, S, D)
    return y * pool_scale


def conv_glu_ffn(h, w_up, conv_w, conv_b, w_down):
    u = h @ w_up
    up = jnp.pad(u, ((0, 0), (CONV_WIDTH - 1, 0), (0, 0)))
    S = h.shape[1]
    u = conv_b + conv_w[0] * up[:, 0:S] + conv_w[1] * up[:, 1:S + 1] + conv_w[2] * up[:, 2:S + 2]
    gate, val = u[..., :D_FF], u[..., D_FF:]
    return (jax.nn.silu(gate) * val) @ w_down


def setup_inputs(seed: int = 0) -> dict:
    key = jax.random.key(seed)
    ks = jax.random.split(key, 20)
    D = D_MODEL
    nrm = jax.random.normal
    f32 = jnp.float32
    return {
        "x": nrm(ks[0], (BATCH, SEQ, D), f32),
        "c": nrm(ks[1], (BATCH, D), f32),
        "mod_w": nrm(ks[2], (DEPTH, D, 6 * D), f32) * (0.5 * D ** -0.5),
        "mod_b": nrm(ks[3], (DEPTH, 6 * D), f32) * 0.02,
        "mix_norm_gain": 1.0 + 0.05 * nrm(ks[4], (DEPTH, D), f32),
        "ffn_norm_gain": 1.0 + 0.05 * nrm(ks[5], (DEPTH, D), f32),
        "w_qkv": nrm(ks[6], (N_ATTN_LAYERS, D, QKV_DIM), f32) * D ** -0.5,
        "q_gain": 1.0 + 0.05 * nrm(ks[7], (N_ATTN_LAYERS, HEAD_DIM), f32),
        "k_gain": 1.0 + 0.05 * nrm(ks[8], (N_ATTN_LAYERS, HEAD_DIM), f32),
        "sinks": nrm(ks[9], (N_ATTN_LAYERS, N_Q_HEADS), f32),
        "w_o": nrm(ks[10], (N_ATTN_LAYERS, N_Q_HEADS * HEAD_DIM, D), f32) * (N_Q_HEADS * HEAD_DIM) ** -0.5,
        "pool_w": nrm(ks[11], (N_POOL_LAYERS, N_POOL_GROUPS, POOL_GROUP_DIM, POOL_GROUP_DIM), f32) * POOL_GROUP_DIM ** -0.5,
        "pool_scale": 1.0 + 0.1 * nrm(ks[12], (N_POOL_LAYERS, D), f32),
        "w_up": nrm(ks[13], (DEPTH, D, 2 * D_FF), f32) * D ** -0.5,
        "conv_w": nrm(ks[14], (DEPTH, CONV_WIDTH, 2 * D_FF), f32) * CONV_WIDTH ** -0.5,
        "conv_b": nrm(ks[15], (DEPTH, 2 * D_FF), f32) * 0.02,
        "w_down": nrm(ks[16], (DEPTH, D_FF, D), f32) * D_FF ** -0.5,
    }


def reference(x, c, mod_w, mod_b, mix_norm_gain, ffn_norm_gain, w_qkv, q_gain, k_gain,
              sinks, w_o, pool_w, pool_scale, w_up, conv_w, conv_b, w_down):
    c_act = jax.nn.silu(c)
    for i in range(DEPTH):
        mod = c_act @ mod_w[i] + mod_b[i]
        sh_m, sc_m, g_m, sh_f, sc_f, g_f = [m[:, None, :] for m in jnp.split(mod, 6, axis=-1)]

        h = rmsnorm(x, mix_norm_gain[i]) * (1.0 + sc_m) + sh_m
        if i % N_MIXERS == 0:
            a = i // N_MIXERS
            y = swa_sink_attention(h, w_qkv[a], q_gain[a], k_gain[a], sinks[a], w_o[a])
        else:
            p = i // N_MIXERS
            y = multiscale_pool_mixer(h, pool_w[p], pool_scale[p])
        x = x + g_m * y

        h = rmsnorm(x, ffn_norm_gain[i]) * (1.0 + sc_f) + sh_f
        x = x + g_f * conv_glu_ffn(h, w_up[i], conv_w[i], conv_b[i], w_down[i])
    return x
```

```python
import functools
import math

import jax
import jax.numpy as jnp
from jax import lax
from jax.experimental import pallas as pl
from jax.experimental.pallas import tpu as pltpu

D_MODEL = 1024
DEPTH = 2
HEAD_DIM = 64
N_Q_HEADS = 16
N_KV_HEADS = 4
GROUP = 4
Q_DIM = N_Q_HEADS * HEAD_DIM
KV_DIM = N_KV_HEADS * HEAD_DIM
QKV_DIM = Q_DIM + 2 * KV_DIM
BLOCK = 128
ROPE_DIM = HEAD_DIM // 4
ROPE_THETA = 500000.0
POOL_WINDOWS = (2, 4, 8, 16)
POOL_GROUP_DIM = D_MODEL // len(POOL_WINDOWS)
MAX_POOL_WINDOW = max(POOL_WINDOWS)
D_FF = 2816
NORM_EPS = 1e-6
NEG_INF = -1e30

LANES = 128
MXU_DIM = 256
VMEM_LIMIT_BYTES = 56 * 1024 * 1024

MOD_TN = 1024
QKV_TM = 512
ATTN_TQ = 256
FFN_TM = 512
FFN_FC = 256
POOL_TM = 512
CONV_HEAD = 16
CARRY_ROWS = 8

_NT = (((1,), (1,)), ((), ()))


def _bdot(a, b):
    return jnp.dot(a, b, preferred_element_type=jnp.float32)


def _adaln(x, gain, scale, shift):
    ms = jnp.mean(x * x, axis=-1, keepdims=True)
    return (x * lax.rsqrt(ms + NORM_EPS)) * (gain * (1.0 + scale)) + shift


def _mod_kernel(c_ref, w_ref, b_ref, o_ref):
    c = c_ref[...]
    ca = (c * jax.nn.sigmoid(c)).astype(jnp.bfloat16)
    o_ref[...] = _bdot(ca, w_ref[...].astype(jnp.bfloat16)) + b_ref[...]


def _modulation(c_pad, mod_w, mod_b):
    rows = c_pad.shape[0]
    n = mod_w.shape[-1]
    return pl.pallas_call(
        _mod_kernel,
        grid=(DEPTH, n // MOD_TN),
        in_specs=[
            pl.BlockSpec((rows, D_MODEL), lambda l, j: (0, 0)),
            pl.BlockSpec((None, D_MODEL, MOD_TN), lambda l, j: (l, 0, j)),
            pl.BlockSpec((None, 1, MOD_TN), lambda l, j: (l, 0, j)),
        ],
        out_specs=pl.BlockSpec((None, rows, MOD_TN), lambda l, j: (l, 0, j)),
        out_shape=jax.ShapeDtypeStruct((DEPTH, rows, n), jnp.float32),
        compiler_params=pltpu.CompilerParams(
            dimension_semantics=("arbitrary", "arbitrary"),
            vmem_limit_bytes=VMEM_LIMIT_BYTES),
        name="adaln_mod",
    )(c_pad, mod_w, mod_b.reshape(DEPTH, 1, n))


def _dup_halves(col):
    lo = lax.broadcasted_iota(jnp.int32, col.shape, 1) < HEAD_DIM
    r = pltpu.roll(col, HEAD_DIM, 1)
    return jnp.where(lo, col, r), jnp.where(lo, r, col)


def _qkv_kernel(x_ref, mod_ref, gain_ref, w_ref, qkg_ref, rc_ref, rs1_ref, rs2_ref,
                q_ref, kd_ref, vd_ref):
    x = x_ref[...]
    h = _adaln(x, gain_ref[...], mod_ref[1:2, :], mod_ref[0:1, :]).astype(jnp.bfloat16)
    qkv = _bdot(h, w_ref[...])

    r = lax.broadcasted_iota(jnp.int32, (MXU_DIM, MXU_DIM), 0) // HEAD_DIM
    cidx = lax.broadcasted_iota(jnp.int32, (MXU_DIM, MXU_DIM), 1) // HEAD_DIM
    avg = jnp.where(r == cidx, 1.0 / HEAD_DIM, 0.0).astype(jnp.bfloat16)

    rc, rs1, rs2 = rc_ref[...], rs1_ref[...], rs2_ref[...]
    qkg = qkg_ref[...]
    n_chunks = (Q_DIM + KV_DIM) // MXU_DIM
    k_cols = []
    for ci in range(n_chunks):
        t = qkv[:, ci * MXU_DIM:(ci + 1) * MXU_DIM]
        sq = t * t
        sq_hi = sq.astype(jnp.bfloat16)
        sq_lo = (sq - sq_hi.astype(jnp.float32)).astype(jnp.bfloat16)
        ms = _bdot(sq_hi, avg) + _bdot(sq_lo, avg)
        tn = t * lax.rsqrt(ms + NORM_EPS) * qkg[:, ci * MXU_DIM:(ci + 1) * MXU_DIM]
        for half in range(MXU_DIM // LANES):
            u = tn[:, half * LANES:(half + 1) * LANES]
            u = (u * rc + pltpu.roll(u, LANES - ROPE_DIM // 2, 1) * rs1
                 + pltpu.roll(u, ROPE_DIM // 2, 1) * rs2)
            col = ci * MXU_DIM + half * LANES
            if col < Q_DIM:
                q_ref[:, col:col + LANES] = u.astype(jnp.bfloat16)
            else:
                k_cols.append(u)
    for pair, col in enumerate(k_cols):
        a, b = _dup_halves(col)
        kd_ref[2 * pair] = a.astype(jnp.bfloat16)
        kd_ref[2 * pair + 1] = b.astype(jnp.bfloat16)
    for pair in range(KV_DIM // LANES):
        col = qkv[:, Q_DIM + KV_DIM + pair * LANES:Q_DIM + KV_DIM + (pair + 1) * LANES]
        a, b = _dup_halves(col)
        vd_ref[2 * pair] = a.astype(jnp.bfloat16)
        vd_ref[2 * pair + 1] = b.astype(jnp.bfloat16)


def _qkv_call(x2, mod0, gain, w_qkv_bf, qk_gain, rope_c, rope_s1, rope_s2, seq):
    n_tok = x2.shape[0]
    tpb = seq // QKV_TM
    tok = lambda i: (i, 0)
    pos = lambda i: (i % tpb, 0)
    return pl.pallas_call(
        _qkv_kernel,
        grid=(n_tok // QKV_TM,),
        in_specs=[
            pl.BlockSpec((QKV_TM, D_MODEL), tok),
            pl.BlockSpec((None, 6, D_MODEL), lambda i: (i // tpb, 0, 0)),
            pl.BlockSpec((1, D_MODEL), lambda i: (0, 0)),
            pl.BlockSpec((D_MODEL, QKV_DIM), lambda i: (0, 0)),
            pl.BlockSpec((1, Q_DIM + KV_DIM), lambda i: (0, 0)),
            pl.BlockSpec((QKV_TM, LANES), pos),
            pl.BlockSpec((QKV_TM, LANES), pos),
            pl.BlockSpec((QKV_TM, LANES), pos),
        ],
        out_specs=[
            pl.BlockSpec((QKV_TM, Q_DIM), tok),
            pl.BlockSpec((N_KV_HEADS, QKV_TM, LANES), lambda i: (0, i, 0)),
            pl.BlockSpec((N_KV_HEADS, QKV_TM, LANES), lambda i: (0, i, 0)),
        ],
        out_shape=[
            jax.ShapeDtypeStruct((n_tok, Q_DIM), jnp.bfloat16),
            jax.ShapeDtypeStruct((N_KV_HEADS, n_tok, LANES), jnp.bfloat16),
            jax.ShapeDtypeStruct((N_KV_HEADS, n_tok, LANES), jnp.bfloat16),
        ],
        compiler_params=pltpu.CompilerParams(
            dimension_semantics=("arbitrary",), vmem_limit_bytes=VMEM_LIMIT_BYTES),
        name="qkv_norm_rope",
    )(x2, mod0, gain, w_qkv_bf, qk_gain, rope_c, rope_s1, rope_s2)


def _attn_kernel(sink_ref, x_ref, mod_ref, q_ref, k_ref, kp_ref, v_ref, vp_ref, wo_ref,
                 o_ref, attn_ref, *, tiles_per_seq):
    i = pl.program_id(0)
    no_prev_shift = jnp.where((i % tiles_per_seq) == 0, BLOCK, 0)
    bf = jnp.bfloat16
    rows4 = GROUP * BLOCK

    lane = lax.broadcasted_iota(jnp.int32, (BLOCK, LANES), 1)
    lo = lane < HEAD_DIM
    row_q = lax.broadcasted_iota(jnp.int32, (rows4, BLOCK), 0) % BLOCK
    col_k = lax.broadcasted_iota(jnp.int32, (rows4, BLOCK), 1)
    upper = col_k > row_q
    lane2 = lax.broadcasted_iota(jnp.int32, (2 * BLOCK, LANES), 1)
    lo2 = lane2 < HEAD_DIM
    ones_lo = jnp.where(lo2, 1.0, 0.0).astype(bf)
    ones_hi = jnp.where(lo2, 0.0, 1.0).astype(bf)
    zero_bf = jnp.zeros((), bf)

    for blk in range(ATTN_TQ // BLOCK):
        rows = slice(blk * BLOCK, (blk + 1) * BLOCK)
        for h in range(N_KV_HEADS):
            qp01 = q_ref[rows, h * MXU_DIM:h * MXU_DIM + LANES]
            qp23 = q_ref[rows, h * MXU_DIM + LANES:(h + 1) * MXU_DIM]
            qs = jnp.concatenate([
                jnp.where(lo, qp01, zero_bf), jnp.where(lo, zero_bf, qp01),
                jnp.where(lo, qp23, zero_bf), jnp.where(lo, zero_bf, qp23)], axis=0)
            if blk == 0:
                k_prev, v_prev = kp_ref[h], vp_ref[h]
            else:
                prev = slice((blk - 1) * BLOCK, blk * BLOCK)
                k_prev, v_prev = k_ref[h, prev, :], v_ref[h, prev, :]
            k_band = jnp.concatenate([k_prev, k_ref[h, rows, :]], axis=0)
            v_band = jnp.concatenate([v_prev, v_ref[h, rows, :]], axis=0)
            s2 = lax.dot_general(qs, k_band, _NT, preferred_element_type=jnp.float32)
            s_prev, s_cur = s2[:, :BLOCK], s2[:, BLOCK:]
            if blk == 0:
                use_prev = col_k > row_q + no_prev_shift
                s = jnp.where(use_prev, s_prev, jnp.where(upper, NEG_INF, s_cur))
            else:
                s = jnp.where(upper, s_prev, s_cur)
            sink = jnp.concatenate(
                [jnp.full((BLOCK, 1), sink_ref[h * GROUP + g], jnp.float32) for g in range(GROUP)],
                axis=0)
            m = jnp.maximum(jnp.max(s, axis=-1, keepdims=True), sink)
            p = jnp.exp(s - m).astype(bf)
            e_sink = jnp.exp(sink - m)
            p_band = jnp.concatenate(
                [jnp.where(upper, p, zero_bf), jnp.where(upper, zero_bf, p)], axis=1)
            rhs = jnp.concatenate([
                jnp.concatenate([jnp.where(lo2, v_band, zero_bf), ones_lo], axis=1),
                jnp.concatenate([jnp.where(lo2, zero_bf, v_band), ones_hi], axis=1)], axis=0)
            for pair in range(GROUP // 2):
                r0 = 2 * pair * BLOCK
                lhs = jnp.concatenate(
                    [p_band[r0:r0 + BLOCK], p_band[r0 + BLOCK:r0 + 2 * BLOCK]], axis=1)
                o = _bdot(lhs, rhs)
                es = jnp.where(lo, e_sink[r0:r0 + BLOCK], e_sink[r0 + BLOCK:r0 + 2 * BLOCK])
                out = o[:, :LANES] / (o[:, LANES:] + es)
                c0 = h * MXU_DIM + pair * LANES
                attn_ref[rows, c0:c0 + LANES] = out.astype(bf)

    y = _bdot(attn_ref[...], wo_ref[...])
    o_ref[...] = x_ref[...] + mod_ref[2:3, :] * y


def _attn_call(sinks, x2, mod0, q, kd, vd, wo_bf, seq):
    n_tok = x2.shape[0]
    tps = seq // ATTN_TQ
    per = ATTN_TQ // BLOCK
    tok = lambda i: (i, 0)
    cur = lambda i: (0, i, 0)
    prv = lambda i: (0, jnp.maximum(i * per - 1, 0), 0)
    return pl.pallas_call(
        functools.partial(_attn_kernel, tiles_per_seq=tps),
        grid=(n_tok // ATTN_TQ,),
        in_specs=[
            pl.BlockSpec(memory_space=pltpu.SMEM),
            pl.BlockSpec((ATTN_TQ, D_MODEL), tok),
            pl.BlockSpec((None, 6, D_MODEL), lambda i: (i // tps, 0, 0)),
            pl.BlockSpec((ATTN_TQ, Q_DIM), tok),
            pl.BlockSpec((N_KV_HEADS, ATTN_TQ, LANES), cur),
            pl.BlockSpec((N_KV_HEADS, BLOCK, LANES), prv),
            pl.BlockSpec((N_KV_HEADS, ATTN_TQ, LANES), cur),
            pl.BlockSpec((N_KV_HEADS, BLOCK, LANES), prv),
            pl.BlockSpec((Q_DIM, D_MODEL), lambda i: (0, 0)),
        ],
        out_specs=pl.BlockSpec((ATTN_TQ, D_MODEL), tok),
        out_shape=jax.ShapeDtypeStruct((n_tok, D_MODEL), jnp.float32),
        scratch_shapes=[pltpu.VMEM((ATTN_TQ, Q_DIM), jnp.bfloat16)],
        compiler_params=pltpu.CompilerParams(
            dimension_semantics=("arbitrary",), vmem_limit_bytes=VMEM_LIMIT_BYTES),
        name="swa_attn_out",
    )(sinks, x2, mod0, q, kd, kd, vd, vd, wo_bf)


def _pool_kernel(x_ref, mod_ref, gain_ref, pw_ref, ps_ref, o_ref, carry_ref, *, tiles_per_seq):
    i = pl.program_id(0)
    t_in_seq = i % tiles_per_seq

    @pl.when(t_in_seq == 0)
    def _():
        carry_ref[...] = jnp.zeros_like(carry_ref)

    x = x_ref[...]
    h = _adaln(x, gain_ref[...], mod_ref[1:2, :], mod_ref[0:1, :])
    tm = x.shape[0]
    pos1 = (t_in_seq * tm + 1 + lax.broadcasted_iota(jnp.int32, (tm, 1), 0)).astype(jnp.float32)
    ys = []
    for g, w in enumerate(POOL_WINDOWS):
        cols = slice(g * POOL_GROUP_DIM, (g + 1) * POOL_GROUP_DIM)
        hg = h[:, cols]
        ext = jnp.concatenate([carry_ref[:, cols], hg], axis=0)
        acc = ext
        step = 1
        while step < w:
            acc = acc + pltpu.roll(acc, step, 0)
            step *= 2
        win = acc[MAX_POOL_WINDOW:]
        d = win / jnp.minimum(pos1, float(w)) - hg
        ys.append(_bdot(d.astype(jnp.bfloat16), pw_ref[g]))
    carry_ref[...] = h[tm - MAX_POOL_WINDOW:]
    y = jnp.concatenate(ys, axis=1) * ps_ref[...]
    o_ref[...] = x + mod_ref[2:3, :] * y


def _pool_call(x2, mod1, gain, pool_w_bf, pool_scale, seq):
    n_tok = x2.shape[0]
    tps = seq // POOL_TM
    tok = lambda i: (i, 0)
    ng = len(POOL_WINDOWS)
    return pl.pallas_call(
        functools.partial(_pool_kernel, tiles_per_seq=tps),
        grid=(n_tok // POOL_TM,),
        in_specs=[
            pl.BlockSpec((POOL_TM, D_MODEL), tok),
            pl.BlockSpec((None, 6, D_MODEL), lambda i: (i // tps, 0, 0)),
            pl.BlockSpec((1, D_MODEL), lambda i: (0, 0)),
            pl.BlockSpec((ng, POOL_GROUP_DIM, POOL_GROUP_DIM), lambda i: (0, 0, 0)),
            pl.BlockSpec((1, D_MODEL), lambda i: (0, 0)),
        ],
        out_specs=pl.BlockSpec((POOL_TM, D_MODEL), tok),
        out_shape=jax.ShapeDtypeStruct((n_tok, D_MODEL), jnp.float32),
        scratch_shapes=[pltpu.VMEM((MAX_POOL_WINDOW, D_MODEL), jnp.float32)],
        compiler_params=pltpu.CompilerParams(
            dimension_semantics=("arbitrary",), vmem_limit_bytes=VMEM_LIMIT_BYTES),
        name="pool_mixer",
    )(x2, mod1, gain, pool_w_bf, pool_scale)


def _causal_conv(u, carry8, w, b):
    w0, w1, w2 = w[0:1, :], w[1:2, :], w[2:3, :]
    full = b + w2 * u + w1 * pltpu.roll(u, 1, 0) + w0 * pltpu.roll(u, 2, 0)
    uh = u[:CONV_HEAD]
    e = jnp.concatenate([carry8, uh], axis=0)
    e1 = pltpu.roll(e, 1, 0)[CARRY_ROWS:]
    e2 = pltpu.roll(e, 2, 0)[CARRY_ROWS:]
    head = b + w2 * uh + w1 * e1 + w0 * e2
    return full, head


def _ffn_kernel(x_ref, mod_ref, gain_ref, wup_ref, cw_ref, cb_ref, wdown_ref, o_ref,
                carry_ref, act_ref, *, tiles_per_seq):
    i = pl.program_id(0)

    @pl.when(i % tiles_per_seq == 0)
    def _():
        carry_ref[...] = jnp.zeros_like(carry_ref)

    x = x_ref[...]
    tm = x.shape[0]
    h = _adaln(x, gain_ref[...], mod_ref[4:5, :], mod_ref[3:4, :]).astype(jnp.bfloat16)
    for c in range(D_FF // FFN_FC):
        parts = []
        for base in (0, D_FF):
            cols = slice(base + c * FFN_FC, base + (c + 1) * FFN_FC)
            u = _bdot(h, wup_ref[:, cols])
            carry8 = carry_ref[:, cols]
            carry_ref[:, cols] = u[tm - CARRY_ROWS:]
            parts.append(_causal_conv(u, carry8, cw_ref[:, cols], cb_ref[:, cols]))
        (g_full, g_head), (v_full, v_head) = parts
        acols = slice(c * FFN_FC, (c + 1) * FFN_FC)
        act_ref[:, acols] = (g_full * jax.nn.sigmoid(g_full) * v_full).astype(jnp.bfloat16)
        act_ref[0:CONV_HEAD, acols] = (g_head * jax.nn.sigmoid(g_head) * v_head).astype(jnp.bfloat16)
    y = _bdot(act_ref[...], wdown_ref[...])
    o_ref[...] = x + mod_ref[5:6, :] * y


def _resident(shape):
    return pl.BlockSpec(shape, lambda i: (0,) * len(shape), pipeline_mode=pl.Buffered(1))


def _ffn_call(x2, mod_l, gain, wup_bf, conv_w, conv_b, wdown_bf, seq):
    n_tok = x2.shape[0]
    tps = seq // FFN_TM
    tok = lambda i: (i, 0)
    return pl.pallas_call(
        functools.partial(_ffn_kernel, tiles_per_seq=tps),
        grid=(n_tok // FFN_TM,),
        in_specs=[
            pl.BlockSpec((FFN_TM, D_MODEL), tok),
            pl.BlockSpec((None, 6, D_MODEL), lambda i: (i // tps, 0, 0)),
            pl.BlockSpec((1, D_MODEL), lambda i: (0, 0)),
            _resident((D_MODEL, 2 * D_FF)),
            pl.BlockSpec((3, 2 * D_FF), lambda i: (0, 0)),
            pl.BlockSpec((1, 2 * D_FF), lambda i: (0, 0)),
            _resident((D_FF, D_MODEL)),
        ],
        out_specs=pl.BlockSpec((FFN_TM, D_MODEL), tok),
        out_shape=jax.ShapeDtypeStruct((n_tok, D_MODEL), jnp.float32),
        scratch_shapes=[
            pltpu.VMEM((CARRY_ROWS, 2 * D_FF), jnp.float32),
            pltpu.VMEM((FFN_TM, D_FF), jnp.bfloat16),
        ],
        compiler_params=pltpu.CompilerParams(
            dimension_semantics=("arbitrary",), vmem_limit_bytes=VMEM_LIMIT_BYTES),
        name="conv_glu_ffn",
    )(x2, mod_l, gain, wup_bf, conv_w, conv_b, wdown_bf)


def _rope_lane_tables(seq):
    half = ROPE_DIM // 2
    inv_freq = ROPE_THETA ** (-jnp.arange(0, half, dtype=jnp.float32) * 2.0 / ROPE_DIM)
    pos = jnp.arange(seq, dtype=jnp.float32)
    ang = pos[:, None] * inv_freq[None, :]
    cos, sin = jnp.cos(ang), jnp.sin(ang)
    ones = jnp.ones((seq, HEAD_DIM - ROPE_DIM), jnp.float32)
    zeros8 = jnp.zeros((seq, half), jnp.float32)
    zeros = jnp.zeros((seq, HEAD_DIM - ROPE_DIM), jnp.float32)
    c = jnp.concatenate([cos, cos, ones], axis=1)
    s1 = jnp.concatenate([-sin, zeros8, zeros], axis=1)
    s2 = jnp.concatenate([zeros8, sin, zeros], axis=1)
    rep = LANES // HEAD_DIM
    return jnp.tile(c, (1, rep)), jnp.tile(s1, (1, rep)), jnp.tile(s2, (1, rep))


def kernel(x, c, mod_w, mod_b, mix_norm_gain, ffn_norm_gain, w_qkv, q_gain, k_gain, sinks, w_o,
           pool_w, pool_scale, w_up, conv_w, conv_b, w_down):
    batch, seq, d = x.shape
    assert d == D_MODEL and seq % max(QKV_TM, ATTN_TQ, FFN_TM, POOL_TM) == 0
    bf = jnp.bfloat16
    x2 = x.reshape(batch * seq, d)

    c_pad = jnp.pad(c, ((0, 8 - batch), (0, 0)))
    mod = _modulation(c_pad, mod_w, mod_b)[:, :batch]
    mod = mod.reshape(DEPTH, batch, 6, d)

    rope_c, rope_s1, rope_s2 = _rope_lane_tables(seq)
    scale = 1.0 / math.sqrt(HEAD_DIM)
    qk_gain = jnp.concatenate(
        [jnp.tile(q_gain[0] * scale, N_Q_HEADS), jnp.tile(k_gain[0], N_KV_HEADS)])[None, :]

    q, kd, vd = _qkv_call(x2, mod[0], mix_norm_gain[0:1], w_qkv[0].astype(bf), qk_gain,
                          rope_c, rope_s1, rope_s2, seq)
    x2 = _attn_call(sinks[0], x2, mod[0], q, kd, vd, w_o[0].astype(bf), seq)
    x2 = _ffn_call(x2, mod[0], ffn_norm_gain[0:1], w_up[0].astype(bf), conv_w[0], conv_b[0:1],
                   w_down[0].astype(bf), seq)
    x2 = _pool_call(x2, mod[1], mix_norm_gain[1:2], pool_w[0].astype(bf), pool_scale[0:1], seq)
    x2 = _ffn_call(x2, mod[1], ffn_norm_gain[1:2], w_up[1].astype(bf), conv_w[1], conv_b[1:2],
                   w_down[1].astype(bf), seq)
    return x2.reshape(batch, seq, d)
```

```python
import functools
import math

import jax
import jax.numpy as jnp
from jax import lax
from jax.experimental import pallas as pl
from jax.experimental.pallas import tpu as pltpu

D_MODEL = 1024
DEPTH = 2
HEAD_DIM = 64
N_Q_HEADS = 16
N_KV_HEADS = 4
GROUP = 4
Q_DIM = N_Q_HEADS * HEAD_DIM
KV_DIM = N_KV_HEADS * HEAD_DIM
QKV_DIM = Q_DIM + 2 * KV_DIM
BLOCK = 128
ROPE_DIM = HEAD_DIM // 4
ROPE_THETA = 500000.0
POOL_WINDOWS = (2, 4, 8, 16)
POOL_GROUP_DIM = D_MODEL // len(POOL_WINDOWS)
MAX_POOL_WINDOW = max(POOL_WINDOWS)
D_FF = 2816
NORM_EPS = 1e-6
NEG_INF = -1e30

LANES = 128
MXU_DIM = 256
VMEM_LIMIT_BYTES = 56 * 1024 * 1024

MOD_TN = 1024
QKV_TM = 512
ATTN_TQ = 256
FFN_TM = 512
FFN_FC = 256
POOL_TM = 512
CONV_HEAD = 16
CARRY_ROWS = 8

_NT = (((1,), (1,)), ((), ()))


def _bdot(a, b):
    return jnp.dot(a, b, preferred_element_type=jnp.float32)


def _adaln(x, gain, scale, shift):
    ms = jnp.mean(x * x, axis=-1, keepdims=True)
    return (x * lax.rsqrt(ms + NORM_EPS)) * (gain * (1.0 + scale)) + shift


def _mod_kernel(c_ref, w_ref, b_ref, o_ref):
    c = c_ref[...]
    ca = (c * jax.nn.sigmoid(c)).astype(jnp.bfloat16)
    o_ref[...] = _bdot(ca, w_ref[...].astype(jnp.bfloat16)) + b_ref[...]


def _modulation(c_pad, mod_w, mod_b):
    rows = c_pad.shape[0]
    n = mod_w.shape[-1]
    return pl.pallas_call(
        _mod_kernel,
        grid=(DEPTH, n // MOD_TN),
        in_specs=[
            pl.BlockSpec((rows, D_MODEL), lambda l, j: (0, 0)),
            pl.BlockSpec((None, D_MODEL, MOD_TN), lambda l, j: (l, 0, j)),
            pl.BlockSpec((None, 1, MOD_TN), lambda l, j: (l, 0, j)),
        ],
        out_specs=pl.BlockSpec((None, rows, MOD_TN), lambda l, j: (l, 0, j)),
        out_shape=jax.ShapeDtypeStruct((DEPTH, rows, n), jnp.float32),
        compiler_params=pltpu.CompilerParams(
            dimension_semantics=("arbitrary", "arbitrary"),
            vmem_limit_bytes=VMEM_LIMIT_BYTES),
        name="adaln_mod",
    )(c_pad, mod_w, mod_b.reshape(DEPTH, 1, n))


def _dup_halves(col):
    lo = lax.broadcasted_iota(jnp.int32, col.shape, 1) < HEAD_DIM
    r = pltpu.roll(col, HEAD_DIM, 1)
    return jnp.where(lo, col, r), jnp.where(lo, r, col)


def _qkv_kernel(x_ref, mod_ref, gain_ref, w_ref, qkg_ref, rc_ref, rs1_ref, rs2_ref,
                q_ref, kd_ref, vd_ref):
    x = x_ref[...]
    h = _adaln(x, gain_ref[...], mod_ref[1:2, :], mod_ref[0:1, :]).astype(jnp.bfloat16)
    qkv = _bdot(h, w_ref[...])

    r = lax.broadcasted_iota(jnp.int32, (MXU_DIM, MXU_DIM), 0) // HEAD_DIM
    cidx = lax.broadcasted_iota(jnp.int32, (MXU_DIM, MXU_DIM), 1) // HEAD_DIM
    avg = jnp.where(r == cidx, 1.0 / HEAD_DIM, 0.0).astype(jnp.bfloat16)

    rc, rs1, rs2 = rc_ref[...], rs1_ref[...], rs2_ref[...]
    qkg = qkg_ref[...]
    n_chunks = (Q_DIM + KV_DIM) // MXU_DIM
    k_cols = []
    for ci in range(n_chunks):
        t = qkv[:, ci * MXU_DIM:(ci + 1) * MXU_DIM]
        ms = _bdot((t * t).astype(jnp.bfloat16), avg)
        tn = t * lax.rsqrt(ms + NORM_EPS) * qkg[:, ci * MXU_DIM:(ci + 1) * MXU_DIM]
        for half in range(MXU_DIM // LANES):
            u = tn[:, half * LANES:(half + 1) * LANES]
            u = (u * rc + pltpu.roll(u, LANES - ROPE_DIM // 2, 1) * rs1
                 + pltpu.roll(u, ROPE_DIM // 2, 1) * rs2)
            col = ci * MXU_DIM + half * LANES
            if col < Q_DIM:
                q_ref[:, col:col + LANES] = u.astype(jnp.bfloat16)
            else:
                k_cols.append(u)
    for pair, col in enumerate(k_cols):
        a, b = _dup_halves(col)
        kd_ref[2 * pair] = a.astype(jnp.bfloat16)
        kd_ref[2 * pair + 1] = b.astype(jnp.bfloat16)
    for pair in range(KV_DIM // LANES):
        col = qkv[:, Q_DIM + KV_DIM + pair * LANES:Q_DIM + KV_DIM + (pair + 1) * LANES]
        a, b = _dup_halves(col)
        vd_ref[2 * pair] = a.astype(jnp.bfloat16)
        vd_ref[2 * pair + 1] = b.astype(jnp.bfloat16)


def _qkv_call(x2, mod0, gain, w_qkv_bf, qk_gain, rope_c, rope_s1, rope_s2, seq):
    n_tok = x2.shape[0]
    tpb = seq // QKV_TM
    tok = lambda i: (i, 0)
    pos = lambda i: (i % tpb, 0)
    return pl.pallas_call(
        _qkv_kernel,
        grid=(n_tok // QKV_TM,),
        in_specs=[
            pl.BlockSpec((QKV_TM, D_MODEL), tok),
            pl.BlockSpec((None, 6, D_MODEL), lambda i: (i // tpb, 0, 0)),
            pl.BlockSpec((1, D_MODEL), lambda i: (0, 0)),
            pl.BlockSpec((D_MODEL, QKV_DIM), lambda i: (0, 0)),
            pl.BlockSpec((1, Q_DIM + KV_DIM), lambda i: (0, 0)),
            pl.BlockSpec((QKV_TM, LANES), pos),
            pl.BlockSpec((QKV_TM, LANES), pos),
            pl.BlockSpec((QKV_TM, LANES), pos),
        ],
        out_specs=[
            pl.BlockSpec((QKV_TM, Q_DIM), tok),
            pl.BlockSpec((N_KV_HEADS, QKV_TM, LANES), lambda i: (0, i, 0)),
            pl.BlockSpec((N_KV_HEADS, QKV_TM, LANES), lambda i: (0, i, 0)),
        ],
        out_shape=[
            jax.ShapeDtypeStruct((n_tok, Q_DIM), jnp.bfloat16),
            jax.ShapeDtypeStruct((N_KV_HEADS, n_tok, LANES), jnp.bfloat16),
            jax.ShapeDtypeStruct((N_KV_HEADS, n_tok, LANES), jnp.bfloat16),
        ],
        compiler_params=pltpu.CompilerParams(
            dimension_semantics=("arbitrary",), vmem_limit_bytes=VMEM_LIMIT_BYTES),
        name="qkv_norm_rope",
    )(x2, mod0, gain, w_qkv_bf, qk_gain, rope_c, rope_s1, rope_s2)


def _attn_kernel(sink_ref, x_ref, mod_ref, q_ref, k_ref, kp_ref, v_ref, vp_ref, wo_ref,
                 o_ref, attn_ref, *, tiles_per_seq):
    i = pl.program_id(0)
    no_prev_shift = jnp.where((i % tiles_per_seq) == 0, BLOCK, 0)
    bf = jnp.bfloat16
    rows4 = GROUP * BLOCK

    lane = lax.broadcasted_iota(jnp.int32, (BLOCK, LANES), 1)
    lo = lane < HEAD_DIM
    row_q = lax.broadcasted_iota(jnp.int32, (rows4, BLOCK), 0) % BLOCK
    col_k = lax.broadcasted_iota(jnp.int32, (rows4, BLOCK), 1)
    upper = col_k > row_q
    lane2 = lax.broadcasted_iota(jnp.int32, (2 * BLOCK, LANES), 1)
    lo2 = lane2 < HEAD_DIM
    ones_lo = jnp.where(lo2, 1.0, 0.0).astype(bf)
    ones_hi = jnp.where(lo2, 0.0, 1.0).astype(bf)
    zero_bf = jnp.zeros((), bf)

    for blk in range(ATTN_TQ // BLOCK):
        rows = slice(blk * BLOCK, (blk + 1) * BLOCK)
        for h in range(N_KV_HEADS):
            qp01 = q_ref[rows, h * MXU_DIM:h * MXU_DIM + LANES]
            qp23 = q_ref[rows, h * MXU_DIM + LANES:(h + 1) * MXU_DIM]
            qs = jnp.concatenate([
                jnp.where(lo, qp01, zero_bf), jnp.where(lo, zero_bf, qp01),
                jnp.where(lo, qp23, zero_bf), jnp.where(lo, zero_bf, qp23)], axis=0)
            if blk == 0:
                k_prev, v_prev = kp_ref[h], vp_ref[h]
            else:
                prev = slice((blk - 1) * BLOCK, blk * BLOCK)
                k_prev, v_prev = k_ref[h, prev, :], v_ref[h, prev, :]
            k_band = jnp.concatenate([k_prev, k_ref[h, rows, :]], axis=0)
            v_band = jnp.concatenate([v_prev, v_ref[h, rows, :]], axis=0)
            s2 = lax.dot_general(qs, k_band, _NT, preferred_element_type=jnp.float32)
            s_prev, s_cur = s2[:, :BLOCK], s2[:, BLOCK:]
            if blk == 0:
                use_prev = col_k > row_q + no_prev_shift
                s = jnp.where(use_prev, s_prev, jnp.where(upper, NEG_INF, s_cur))
            else:
                s = jnp.where(upper, s_prev, s_cur)
            sink = jnp.concatenate(
                [jnp.full((BLOCK, LANES), sink_ref[h * GROUP + g], jnp.float32)
                 for g in range(GROUP)], axis=0)
            m = jnp.maximum(jnp.max(s, axis=-1, keepdims=True), sink)
            p = jnp.exp(s - m).astype(bf)
            e_sink = jnp.exp(sink - m)
            p_band = jnp.concatenate(
                [jnp.where(upper, p, zero_bf), jnp.where(upper, zero_bf, p)], axis=1)
            rhs = jnp.concatenate([
                jnp.concatenate([jnp.where(lo2, v_band, zero_bf), ones_lo], axis=1),
                jnp.concatenate([jnp.where(lo2, zero_bf, v_band), ones_hi], axis=1)], axis=0)
            for pair in range(GROUP // 2):
                r0 = 2 * pair * BLOCK
                lhs = jnp.concatenate(
                    [p_band[r0:r0 + BLOCK], p_band[r0 + BLOCK:r0 + 2 * BLOCK]], axis=1)
                o = _bdot(lhs, rhs)
                es = jnp.where(lo, e_sink[r0:r0 + BLOCK], e_sink[r0 + BLOCK:r0 + 2 * BLOCK])
                out = o[:, :LANES] / (o[:, LANES:] + es)
                c0 = h * MXU_DIM + pair * LANES
                attn_ref[rows, c0:c0 + LANES] = out.astype(bf)

    y = _bdot(attn_ref[...], wo_ref[...])
    o_ref[...] = x_ref[...] + mod_ref[2:3, :] * y


def _attn_call(sinks, x2, mod0, q, kd, vd, wo_bf, seq):
    n_tok = x2.shape[0]
    tps = seq // ATTN_TQ
    per = ATTN_TQ // BLOCK
    tok = lambda i: (i, 0)
    cur = lambda i: (0, i, 0)
    prv = lambda i: (0, jnp.maximum(i * per - 1, 0), 0)
    return pl.pallas_call(
        functools.partial(_attn_kernel, tiles_per_seq=tps),
        grid=(n_tok // ATTN_TQ,),
        in_specs=[
            pl.BlockSpec(memory_space=pltpu.SMEM),
            pl.BlockSpec((ATTN_TQ, D_MODEL), tok),
            pl.BlockSpec((None, 6, D_MODEL), lambda i: (i // tps, 0, 0)),
            pl.BlockSpec((ATTN_TQ, Q_DIM), tok),
            pl.BlockSpec((N_KV_HEADS, ATTN_TQ, LANES), cur),
            pl.BlockSpec((N_KV_HEADS, BLOCK, LANES), prv),
            pl.BlockSpec((N_KV_HEADS, ATTN_TQ, LANES), cur),
            pl.BlockSpec((N_KV_HEADS, BLOCK, LANES), prv),
            pl.BlockSpec((Q_DIM, D_MODEL), lambda i: (0, 0)),
        ],
        out_specs=pl.BlockSpec((ATTN_TQ, D_MODEL), tok),
        out_shape=jax.ShapeDtypeStruct((n_tok, D_MODEL), jnp.float32),
        scratch_shapes=[pltpu.VMEM((ATTN_TQ, Q_DIM), jnp.bfloat16)],
        compiler_params=pltpu.CompilerParams(
            dimension_semantics=("arbitrary",), vmem_limit_bytes=VMEM_LIMIT_BYTES),
        name="swa_attn_out",
    )(sinks, x2, mod0, q, kd, kd, vd, vd, wo_bf)


def _pool_kernel(x_ref, mod_ref, gain_ref, pw_ref, ps_ref, o_ref, carry_ref, *, tiles_per_seq):
    i = pl.program_id(0)
    t_in_seq = i % tiles_per_seq

    @pl.when(t_in_seq == 0)
    def _():
        carry_ref[...] = jnp.zeros_like(carry_ref)

    x = x_ref[...]
    h = _adaln(x, gain_ref[...], mod_ref[1:2, :], mod_ref[0:1, :])
    tm = x.shape[0]
    pos1 = (t_in_seq * tm + 1 + lax.broadcasted_iota(jnp.int32, (tm, 1), 0)).astype(jnp.float32)
    ys = []
    for g, w in enumerate(POOL_WINDOWS):
        cols = slice(g * POOL_GROUP_DIM, (g + 1) * POOL_GROUP_DIM)
        hg = h[:, cols]
        ext = jnp.concatenate([carry_ref[:, cols], hg], axis=0)
        acc = ext
        step = 1
        while step < w:
            acc = acc + pltpu.roll(acc, step, 0)
            step *= 2
        win = acc[MAX_POOL_WINDOW:]
        d = win / jnp.minimum(pos1, float(w)) - hg
        ys.append(_bdot(d.astype(jnp.bfloat16), pw_ref[g]))
    carry_ref[...] = h[tm - MAX_POOL_WINDOW:]
    y = jnp.concatenate(ys, axis=1) * ps_ref[...]
    o_ref[...] = x + mod_ref[2:3, :] * y


def _pool_call(x2, mod1, gain, pool_w_bf, pool_scale, seq):
    n_tok = x2.shape[0]
    tps = seq // POOL_TM
    tok = lambda i: (i, 0)
    ng = len(POOL_WINDOWS)
    return pl.pallas_call(
        functools.partial(_pool_kernel, tiles_per_seq=tps),
        grid=(n_tok // POOL_TM,),
        in_specs=[
            pl.BlockSpec((POOL_TM, D_MODEL), tok),
            pl.BlockSpec((None, 6, D_MODEL), lambda i: (i // tps, 0, 0)),
            pl.BlockSpec((1, D_MODEL), lambda i: (0, 0)),
            pl.BlockSpec((ng, POOL_GROUP_DIM, POOL_GROUP_DIM), lambda i: (0, 0, 0)),
            pl.BlockSpec((1, D_MODEL), lambda i: (0, 0)),
        ],
        out_specs=pl.BlockSpec((POOL_TM, D_MODEL), tok),
        out_shape=jax.ShapeDtypeStruct((n_tok, D_MODEL), jnp.float32),
        scratch_shapes=[pltpu.VMEM((MAX_POOL_WINDOW, D_MODEL), jnp.float32)],
        compiler_params=pltpu.CompilerParams(
            dimension_semantics=("arbitrary",), vmem_limit_bytes=VMEM_LIMIT_BYTES),
        name="pool_mixer",
    )(x2, mod1, gain, pool_w_bf, pool_scale)


def _causal_conv(u, carry8, w, b):
    w0, w1, w2 = w[0:1, :], w[1:2, :], w[2:3, :]
    full = b + w2 * u + w1 * pltpu.roll(u, 1, 0) + w0 * pltpu.roll(u, 2, 0)
    uh = u[:CONV_HEAD]
    e = jnp.concatenate([carry8, uh], axis=0)
    e1 = pltpu.roll(e, 1, 0)[CARRY_ROWS:]
    e2 = pltpu.roll(e, 2, 0)[CARRY_ROWS:]
    head = b + w2 * uh + w1 * e1 + w0 * e2
    return full, head


def _ffn_kernel(x_ref, mod_ref, gain_ref, wup_ref, cw_ref, cb_ref, wdown_ref, o_ref,
                carry_ref, act_ref, *, tiles_per_seq):
    i = pl.program_id(0)

    @pl.when(i % tiles_per_seq == 0)
    def _():
        carry_ref[...] = jnp.zeros_like(carry_ref)

    x = x_ref[...]
    tm = x.shape[0]
    h = _adaln(x, gain_ref[...], mod_ref[4:5, :], mod_ref[3:4, :]).astype(jnp.bfloat16)
    for c in range(D_FF // FFN_FC):
        parts = []
        for base in (0, D_FF):
            cols = slice(base + c * FFN_FC, base + (c + 1) * FFN_FC)
            u = _bdot(h, wup_ref[:, cols])
            carry8 = carry_ref[:, cols]
            carry_ref[:, cols] = u[tm - CARRY_ROWS:]
            parts.append(_causal_conv(u, carry8, cw_ref[:, cols], cb_ref[:, cols]))
        (g_full, g_head), (v_full, v_head) = parts
        acols = slice(c * FFN_FC, (c + 1) * FFN_FC)
        act_ref[:, acols] = (g_full * jax.nn.sigmoid(g_full) * v_full).astype(jnp.bfloat16)
        act_ref[0:CONV_HEAD, acols] = (g_head * jax.nn.sigmoid(g_head) * v_head).astype(jnp.bfloat16)
    y = _bdot(act_ref[...], wdown_ref[...])
    o_ref[...] = x + mod_ref[5:6, :] * y


def _ffn_call(layer, x2, mod, gain, wup_bf, conv_w, conv_b, wdown_bf, seq):
    n_tok = x2.shape[0]
    tps = seq // FFN_TM
    tok = lambda i: (i, 0)
    slab = lambda i: (layer, 0, 0)
    return pl.pallas_call(
        functools.partial(_ffn_kernel, tiles_per_seq=tps),
        grid=(n_tok // FFN_TM,),
        in_specs=[
            pl.BlockSpec((FFN_TM, D_MODEL), tok),
            pl.BlockSpec((None, None, 6, D_MODEL), lambda i: (layer, i // tps, 0, 0)),
            pl.BlockSpec((None, 1, D_MODEL), slab),
            pl.BlockSpec((None, D_MODEL, 2 * D_FF), slab, pipeline_mode=pl.Buffered(1)),
            pl.BlockSpec((None, 3, 2 * D_FF), slab),
            pl.BlockSpec((None, 1, 2 * D_FF), slab),
            pl.BlockSpec((None, D_FF, D_MODEL), slab, pipeline_mode=pl.Buffered(1)),
        ],
        out_specs=pl.BlockSpec((FFN_TM, D_MODEL), tok),
        out_shape=jax.ShapeDtypeStruct((n_tok, D_MODEL), jnp.float32),
        scratch_shapes=[
            pltpu.VMEM((CARRY_ROWS, 2 * D_FF), jnp.float32),
            pltpu.VMEM((FFN_TM, D_FF), jnp.bfloat16),
        ],
        compiler_params=pltpu.CompilerParams(
            dimension_semantics=("arbitrary",), vmem_limit_bytes=VMEM_LIMIT_BYTES),
        name="conv_glu_ffn",
    )(x2, mod, gain, wup_bf, conv_w, conv_b, wdown_bf)


def _rope_lane_tables(seq):
    half = ROPE_DIM // 2
    inv_freq = ROPE_THETA ** (-jnp.arange(0, half, dtype=jnp.float32) * 2.0 / ROPE_DIM)
    pos = jnp.arange(seq, dtype=jnp.float32)
    inv_lane = jnp.tile(jnp.pad(jnp.tile(inv_freq, 2), (0, HEAD_DIM - ROPE_DIM)), LANES // HEAD_DIM)
    ang = pos[:, None] * inv_lane[None, :]
    d = jnp.arange(LANES) % HEAD_DIM
    first = (d < half).astype(jnp.float32)[None, :]
    second = ((d >= half) & (d < ROPE_DIM)).astype(jnp.float32)[None, :]
    sin = jnp.sin(ang)
    return jnp.cos(ang), -sin * first, sin * second


def kernel(x, c, mod_w, mod_b, mix_norm_gain, ffn_norm_gain, w_qkv, q_gain, k_gain, sinks, w_o,
           pool_w, pool_scale, w_up, conv_w, conv_b, w_down):
    batch, seq, d = x.shape
    assert d == D_MODEL and seq % max(QKV_TM, ATTN_TQ, FFN_TM, POOL_TM) == 0
    bf = jnp.bfloat16
    x2 = x.reshape(batch * seq, d)

    c_pad = jnp.pad(c, ((0, 8 - batch), (0, 0)))
    mod = _modulation(c_pad, mod_w, mod_b)[:, :batch]
    mod = mod.reshape(DEPTH, batch, 6, d)

    rope_c, rope_s1, rope_s2 = _rope_lane_tables(seq)
    scale = 1.0 / math.sqrt(HEAD_DIM)
    qk_gain = jnp.concatenate(
        [jnp.tile(q_gain[0] * scale, N_Q_HEADS), jnp.tile(k_gain[0], N_KV_HEADS)])[None, :]

    q, kd, vd = _qkv_call(x2, mod[0], mix_norm_gain[0:1], w_qkv[0].astype(bf), qk_gain,
                          rope_c, rope_s1, rope_s2, seq)
    x2 = _attn_call(sinks[0], x2, mod[0], q, kd, vd, w_o[0].astype(bf), seq)
    ffn_args = (mod, ffn_norm_gain[:, None, :], w_up.astype(bf), conv_w, conv_b[:, None, :],
                w_down.astype(bf), seq)
    x2 = _ffn_call(0, x2, *ffn_args)
    x2 = _pool_call(x2, mod[1], mix_norm_gain[1:2], pool_w[0].astype(bf), pool_scale[0:1], seq)
    x2 = _ffn_call(1, x2, *ffn_args)
    return x2.reshape(batch, seq, d)
```

```python
import functools
import math

import jax
import jax.numpy as jnp
from jax import lax
from jax.experimental import pallas as pl
from jax.experimental.pallas import tpu as pltpu

D_MODEL = 1024
DEPTH = 2
HEAD_DIM = 64
N_Q_HEADS = 16
N_KV_HEADS = 4
GROUP = 4
Q_DIM = N_Q_HEADS * HEAD_DIM
KV_DIM = N_KV_HEADS * HEAD_DIM
QKV_DIM = Q_DIM + 2 * KV_DIM
BLOCK = 128
ROPE_DIM = HEAD_DIM // 4
ROPE_THETA = 500000.0
POOL_WINDOWS = (2, 4, 8, 16)
POOL_GROUP_DIM = D_MODEL // len(POOL_WINDOWS)
MAX_POOL_WINDOW = max(POOL_WINDOWS)
D_FF = 2816
NORM_EPS = 1e-6
NEG_INF = -1e30

LANES = 128
MXU_DIM = 256
VMEM_LIMIT_BYTES = 56 * 1024 * 1024

MOD_TN = 1024
QKV_TM = 1024
QKV_SUB = 256
ATTN_TQ = 512
FFN_TM = 512
FFN_FC = 256
POOL_TM = 512
CONV_HEAD = 16
CARRY_ROWS = 8

_NT = (((1,), (1,)), ((), ()))


def _bdot(a, b):
    return jnp.dot(a, b, preferred_element_type=jnp.float32)


def _adaln(x, gain, scale, shift):
    ms = jnp.mean(x * x, axis=-1, keepdims=True)
    return (x * lax.rsqrt(ms + NORM_EPS)) * (gain * (1.0 + scale)) + shift


def _mod_kernel(c_ref, w_ref, b_ref, o_ref):
    c = c_ref[...]
    ca = (c * jax.nn.sigmoid(c)).astype(jnp.bfloat16)
    o_ref[...] = _bdot(ca, w_ref[...].astype(jnp.bfloat16)) + b_ref[...]


def _modulation(c_pad, mod_w, mod_b):
    rows = c_pad.shape[0]
    n = mod_w.shape[-1]
    return pl.pallas_call(
        _mod_kernel,
        grid=(DEPTH, n // MOD_TN),
        in_specs=[
            pl.BlockSpec((rows, D_MODEL), lambda l, j: (0, 0)),
            pl.BlockSpec((None, D_MODEL, MOD_TN), lambda l, j: (l, 0, j)),
            pl.BlockSpec((None, 1, MOD_TN), lambda l, j: (l, 0, j)),
        ],
        out_specs=pl.BlockSpec((None, rows, MOD_TN), lambda l, j: (l, 0, j)),
        out_shape=jax.ShapeDtypeStruct((DEPTH, rows, n), jnp.float32),
        compiler_params=pltpu.CompilerParams(
            dimension_semantics=("arbitrary", "arbitrary"),
            vmem_limit_bytes=VMEM_LIMIT_BYTES),
        name="adaln_mod",
    )(c_pad, mod_w, mod_b.reshape(DEPTH, 1, n))


def _dup_halves(col):
    lo = lax.broadcasted_iota(jnp.int32, col.shape, 1) < HEAD_DIM
    r = pltpu.roll(col, HEAD_DIM, 1)
    return jnp.where(lo, col, r), jnp.where(lo, r, col)


def _qkv_kernel(x_ref, mod_ref, gain_ref, w_ref, qkg_ref, rot_ref, base_ref,
                q_ref, kd_ref, vd_ref):
    d = lax.broadcasted_iota(jnp.int32, (1, LANES), 1) % HEAD_DIM
    first = jnp.where(d < ROPE_DIM // 2, -1.0, 0.0)
    second = jnp.where((d >= ROPE_DIM // 2) & (d < ROPE_DIM), 1.0, 0.0)
    cos_b, sin_b = base_ref[0:1, :], base_ref[1:2, :]
    r = lax.broadcasted_iota(jnp.int32, (MXU_DIM, MXU_DIM), 0) // HEAD_DIM
    cidx = lax.broadcasted_iota(jnp.int32, (MXU_DIM, MXU_DIM), 1) // HEAD_DIM
    avg = jnp.where(r == cidx, 1.0 / HEAD_DIM, 0.0).astype(jnp.bfloat16)
    qkg = qkg_ref[...]
    gain, scale, shift = gain_ref[...], mod_ref[1:2, :], mod_ref[0:1, :]
    n_chunks = (Q_DIM + KV_DIM) // MXU_DIM

    for sub in range(QKV_TM // QKV_SUB):
        rows = slice(sub * QKV_SUB, (sub + 1) * QKV_SUB)
        h = _adaln(x_ref[rows, :], gain, scale, shift).astype(jnp.bfloat16)
        qkv = _bdot(h, w_ref[...])
        cos_r, sin_r = rot_ref[0, rows, :], rot_ref[1, rows, :]
        rc = cos_b * cos_r - sin_b * sin_r
        sin = sin_b * cos_r + cos_b * sin_r
        rs1, rs2 = sin * first, sin * second
        k_cols = []
        for ci in range(n_chunks):
            t = qkv[:, ci * MXU_DIM:(ci + 1) * MXU_DIM]
            ms = _bdot((t * t).astype(jnp.bfloat16), avg)
            tn = t * lax.rsqrt(ms + NORM_EPS) * qkg[:, ci * MXU_DIM:(ci + 1) * MXU_DIM]
            for half in range(MXU_DIM // LANES):
                u = tn[:, half * LANES:(half + 1) * LANES]
                u = (u * rc + pltpu.roll(u, LANES - ROPE_DIM // 2, 1) * rs1
                     + pltpu.roll(u, ROPE_DIM // 2, 1) * rs2)
                col = ci * MXU_DIM + half * LANES
                if col < Q_DIM:
                    q_ref[rows, col:col + LANES] = u.astype(jnp.bfloat16)
                else:
                    k_cols.append(u)
        for pair, col in enumerate(k_cols):
            a, b = _dup_halves(col)
            kd_ref[2 * pair, rows, :] = a.astype(jnp.bfloat16)
            kd_ref[2 * pair + 1, rows, :] = b.astype(jnp.bfloat16)
        for pair in range(KV_DIM // LANES):
            col = qkv[:, Q_DIM + KV_DIM + pair * LANES:Q_DIM + KV_DIM + (pair + 1) * LANES]
            a, b = _dup_halves(col)
            vd_ref[2 * pair, rows, :] = a.astype(jnp.bfloat16)
            vd_ref[2 * pair + 1, rows, :] = b.astype(jnp.bfloat16)


def _qkv_call(x2, mod0, gain, w_qkv_bf, qk_gain, rope_rot, rope_base, seq):
    n_tok = x2.shape[0]
    tpb = seq // QKV_TM
    tok = lambda i: (i, 0)
    return pl.pallas_call(
        _qkv_kernel,
        grid=(n_tok // QKV_TM,),
        in_specs=[
            pl.BlockSpec((QKV_TM, D_MODEL), tok),
            pl.BlockSpec((None, 6, D_MODEL), lambda i: (i // tpb, 0, 0)),
            pl.BlockSpec((1, D_MODEL), lambda i: (0, 0)),
            pl.BlockSpec((D_MODEL, QKV_DIM), lambda i: (0, 0)),
            pl.BlockSpec((1, Q_DIM + KV_DIM), lambda i: (0, 0)),
            pl.BlockSpec((2, QKV_TM, LANES), lambda i: (0, 0, 0)),
            pl.BlockSpec((None, 2, LANES), lambda i: (i % tpb, 0, 0)),
        ],
        out_specs=[
            pl.BlockSpec((QKV_TM, Q_DIM), tok),
            pl.BlockSpec((N_KV_HEADS, QKV_TM, LANES), lambda i: (0, i, 0)),
            pl.BlockSpec((N_KV_HEADS, QKV_TM, LANES), lambda i: (0, i, 0)),
        ],
        out_shape=[
            jax.ShapeDtypeStruct((n_tok, Q_DIM), jnp.bfloat16),
            jax.ShapeDtypeStruct((N_KV_HEADS, n_tok, LANES), jnp.bfloat16),
            jax.ShapeDtypeStruct((N_KV_HEADS, n_tok, LANES), jnp.bfloat16),
        ],
        compiler_params=pltpu.CompilerParams(
            dimension_semantics=("arbitrary",), vmem_limit_bytes=VMEM_LIMIT_BYTES),
        name="qkv_norm_rope",
    )(x2, mod0, gain, w_qkv_bf, qk_gain, rope_rot, rope_base)


def _attn_kernel(sink_ref, x_ref, mod_ref, q_ref, k_ref, kp_ref, v_ref, vp_ref, wo_ref,
                 o_ref, attn_ref, *, tiles_per_seq):
    i = pl.program_id(0)
    no_prev_shift = jnp.where((i % tiles_per_seq) == 0, BLOCK, 0)
    bf = jnp.bfloat16
    rows4 = GROUP * BLOCK

    lane = lax.broadcasted_iota(jnp.int32, (BLOCK, LANES), 1)
    lo = lane < HEAD_DIM
    row_q = lax.broadcasted_iota(jnp.int32, (rows4, BLOCK), 0) % BLOCK
    col_k = lax.broadcasted_iota(jnp.int32, (rows4, BLOCK), 1)
    upper = col_k > row_q
    lane2 = lax.broadcasted_iota(jnp.int32, (2 * BLOCK, LANES), 1)
    lo2 = lane2 < HEAD_DIM
    ones_lo = jnp.where(lo2, 1.0, 0.0).astype(bf)
    ones_hi = jnp.where(lo2, 0.0, 1.0).astype(bf)
    zero_bf = jnp.zeros((), bf)

    for blk in range(ATTN_TQ // BLOCK):
        rows = slice(blk * BLOCK, (blk + 1) * BLOCK)
        for h in range(N_KV_HEADS):
            qp01 = q_ref[rows, h * MXU_DIM:h * MXU_DIM + LANES]
            qp23 = q_ref[rows, h * MXU_DIM + LANES:(h + 1) * MXU_DIM]
            qs = jnp.concatenate([
                jnp.where(lo, qp01, zero_bf), jnp.where(lo, zero_bf, qp01),
                jnp.where(lo, qp23, zero_bf), jnp.where(lo, zero_bf, qp23)], axis=0)
            if blk == 0:
                k_prev, v_prev = kp_ref[h], vp_ref[h]
            else:
                prev = slice((blk - 1) * BLOCK, blk * BLOCK)
                k_prev, v_prev = k_ref[h, prev, :], v_ref[h, prev, :]
            k_band = jnp.concatenate([k_prev, k_ref[h, rows, :]], axis=0)
            v_band = jnp.concatenate([v_prev, v_ref[h, rows, :]], axis=0)
            s2 = lax.dot_general(qs, k_band, _NT, preferred_element_type=jnp.float32)
            s_prev, s_cur = s2[:, :BLOCK], s2[:, BLOCK:]
            if blk == 0:
                use_prev = col_k > row_q + no_prev_shift
                s = jnp.where(use_prev, s_prev, jnp.where(upper, NEG_INF, s_cur))
            else:
                s = jnp.where(upper, s_prev, s_cur)
            sink = jnp.concatenate(
                [jnp.full((BLOCK, LANES), sink_ref[h * GROUP + g], jnp.float32)
                 for g in range(GROUP)], axis=0)
            m = jnp.maximum(jnp.max(s, axis=-1, keepdims=True), sink)
            p = jnp.exp(s - m).astype(bf)
            e_sink = jnp.exp(sink - m)
            p_band = jnp.concatenate(
                [jnp.where(upper, p, zero_bf), jnp.where(upper, zero_bf, p)], axis=1)
            rhs = jnp.concatenate([
                jnp.concatenate([jnp.where(lo2, v_band, zero_bf), ones_lo], axis=1),
                jnp.concatenate([jnp.where(lo2, zero_bf, v_band), ones_hi], axis=1)], axis=0)
            for pair in range(GROUP // 2):
                r0 = 2 * pair * BLOCK
                lhs = jnp.concatenate(
                    [p_band[r0:r0 + BLOCK], p_band[r0 + BLOCK:r0 + 2 * BLOCK]], axis=1)
                o = _bdot(lhs, rhs)
                es = jnp.where(lo, e_sink[r0:r0 + BLOCK], e_sink[r0 + BLOCK:r0 + 2 * BLOCK])
                out = o[:, :LANES] / (o[:, LANES:] + es)
                c0 = h * MXU_DIM + pair * LANES
                attn_ref[rows, c0:c0 + LANES] = out.astype(bf)

    y = _bdot(attn_ref[...], wo_ref[...])
    o_ref[...] = x_ref[...] + mod_ref[2:3, :] * y


def _attn_call(sinks, x2, mod0, q, kd, vd, wo_bf, seq):
    n_tok = x2.shape[0]
    tps = seq // ATTN_TQ
    per = ATTN_TQ // BLOCK
    tok = lambda i: (i, 0)
    cur = lambda i: (0, i, 0)
    prv = lambda i: (0, jnp.maximum(i * per - 1, 0), 0)
    return pl.pallas_call(
        functools.partial(_attn_kernel, tiles_per_seq=tps),
        grid=(n_tok // ATTN_TQ,),
        in_specs=[
            pl.BlockSpec(memory_space=pltpu.SMEM),
            pl.BlockSpec((ATTN_TQ, D_MODEL), tok),
            pl.BlockSpec((None, 6, D_MODEL), lambda i: (i // tps, 0, 0)),
            pl.BlockSpec((ATTN_TQ, Q_DIM), tok),
            pl.BlockSpec((N_KV_HEADS, ATTN_TQ, LANES), cur),
            pl.BlockSpec((N_KV_HEADS, BLOCK, LANES), prv),
            pl.BlockSpec((N_KV_HEADS, ATTN_TQ, LANES), cur),
            pl.BlockSpec((N_KV_HEADS, BLOCK, LANES), prv),
            pl.BlockSpec((Q_DIM, D_MODEL), lambda i: (0, 0)),
        ],
        out_specs=pl.BlockSpec((ATTN_TQ, D_MODEL), tok),
        out_shape=jax.ShapeDtypeStruct((n_tok, D_MODEL), jnp.float32),
        scratch_shapes=[pltpu.VMEM((ATTN_TQ, Q_DIM), jnp.bfloat16)],
        compiler_params=pltpu.CompilerParams(
            dimension_semantics=("arbitrary",), vmem_limit_bytes=VMEM_LIMIT_BYTES),
        name="swa_attn_out",
    )(sinks, x2, mod0, q, kd, kd, vd, vd, wo_bf)


def _pool_kernel(x_ref, mod_ref, gain_ref, pw_ref, ps_ref, o_ref, carry_ref, *, tiles_per_seq):
    i = pl.program_id(0)
    t_in_seq = i % tiles_per_seq

    @pl.when(t_in_seq == 0)
    def _():
        carry_ref[...] = jnp.zeros_like(carry_ref)

    x = x_ref[...]
    h = _adaln(x, gain_ref[...], mod_ref[1:2, :], mod_ref[0:1, :])
    tm = x.shape[0]
    pos1 = (t_in_seq * tm + 1 + lax.broadcasted_iota(jnp.int32, (tm, 1), 0)).astype(jnp.float32)
    ys = []
    for g, w in enumerate(POOL_WINDOWS):
        cols = slice(g * POOL_GROUP_DIM, (g + 1) * POOL_GROUP_DIM)
        hg = h[:, cols]
        ext = jnp.concatenate([carry_ref[:, cols], hg], axis=0)
        acc = ext
        step = 1
        while step < w:
            acc = acc + pltpu.roll(acc, step, 0)
            step *= 2
        win = acc[MAX_POOL_WINDOW:]
        d = win / jnp.minimum(pos1, float(w)) - hg
        ys.append(_bdot(d.astype(jnp.bfloat16), pw_ref[g]))
    carry_ref[...] = h[tm - MAX_POOL_WINDOW:]
    y = jnp.concatenate(ys, axis=1) * ps_ref[...]
    o_ref[...] = x + mod_ref[2:3, :] * y


def _pool_call(x2, mod1, gain, pool_w_bf, pool_scale, seq):
    n_tok = x2.shape[0]
    tps = seq // POOL_TM
    tok = lambda i: (i, 0)
    ng = len(POOL_WINDOWS)
    return pl.pallas_call(
        functools.partial(_pool_kernel, tiles_per_seq=tps),
        grid=(n_tok // POOL_TM,),
        in_specs=[
            pl.BlockSpec((POOL_TM, D_MODEL), tok),
            pl.BlockSpec((None, 6, D_MODEL), lambda i: (i // tps, 0, 0)),
            pl.BlockSpec((1, D_MODEL), lambda i: (0, 0)),
            pl.BlockSpec((ng, POOL_GROUP_DIM, POOL_GROUP_DIM), lambda i: (0, 0, 0)),
            pl.BlockSpec((1, D_MODEL), lambda i: (0, 0)),
        ],
        out_specs=pl.BlockSpec((POOL_TM, D_MODEL), tok),
        out_shape=jax.ShapeDtypeStruct((n_tok, D_MODEL), jnp.float32),
        scratch_shapes=[pltpu.VMEM((MAX_POOL_WINDOW, D_MODEL), jnp.float32)],
        compiler_params=pltpu.CompilerParams(
            dimension_semantics=("arbitrary",), vmem_limit_bytes=VMEM_LIMIT_BYTES),
        name="pool_mixer",
    )(x2, mod1, gain, pool_w_bf, pool_scale)


def _causal_conv(u, carry8, w, b):
    w0, w1, w2 = w[0:1, :], w[1:2, :], w[2:3, :]
    full = b + w2 * u + w1 * pltpu.roll(u, 1, 0) + w0 * pltpu.roll(u, 2, 0)
    uh = u[:CONV_HEAD]
    e = jnp.concatenate([carry8, uh], axis=0)
    e1 = pltpu.roll(e, 1, 0)[CARRY_ROWS:]
    e2 = pltpu.roll(e, 2, 0)[CARRY_ROWS:]
    head = b + w2 * uh + w1 * e1 + w0 * e2
    return full, head


def _ffn_kernel(x_ref, mod_ref, gain_ref, wup_ref, cw_ref, cb_ref, wdown_ref, o_ref,
                carry_ref, act_ref, *, tiles_per_seq):
    i = pl.program_id(0)

    @pl.when(i % tiles_per_seq == 0)
    def _():
        carry_ref[...] = jnp.zeros_like(carry_ref)

    x = x_ref[...]
    tm = x.shape[0]
    h = _adaln(x, gain_ref[...], mod_ref[4:5, :], mod_ref[3:4, :]).astype(jnp.bfloat16)
    for c in range(D_FF // FFN_FC):
        parts = []
        for base in (0, D_FF):
            cols = slice(base + c * FFN_FC, base + (c + 1) * FFN_FC)
            u = _bdot(h, wup_ref[:, cols])
            carry8 = carry_ref[:, cols]
            carry_ref[:, cols] = u[tm - CARRY_ROWS:]
            parts.append(_causal_conv(u, carry8, cw_ref[:, cols], cb_ref[:, cols]))
        (g_full, g_head), (v_full, v_head) = parts
        acols = slice(c * FFN_FC, (c + 1) * FFN_FC)
        act_ref[:, acols] = (g_full * jax.nn.sigmoid(g_full) * v_full).astype(jnp.bfloat16)
        act_ref[0:CONV_HEAD, acols] = (g_head * jax.nn.sigmoid(g_head) * v_head).astype(jnp.bfloat16)
    y = _bdot(act_ref[...], wdown_ref[...])
    o_ref[...] = x + mod_ref[5:6, :] * y


def _ffn_call(layer, x2, mod, gain, wup_bf, conv_w, conv_b, wdown_bf, seq):
    n_tok = x2.shape[0]
    tps = seq // FFN_TM
    tok = lambda i: (i, 0)
    slab = lambda i: (layer, 0, 0)
    return pl.pallas_call(
        functools.partial(_ffn_kernel, tiles_per_seq=tps),
        grid=(n_tok // FFN_TM,),
        in_specs=[
            pl.BlockSpec((FFN_TM, D_MODEL), tok),
            pl.BlockSpec((None, None, 6, D_MODEL), lambda i: (layer, i // tps, 0, 0)),
            pl.BlockSpec((None, 1, D_MODEL), slab),
            pl.BlockSpec((None, D_MODEL, 2 * D_FF), slab, pipeline_mode=pl.Buffered(1)),
            pl.BlockSpec((None, 3, 2 * D_FF), slab),
            pl.BlockSpec((None, 1, 2 * D_FF), slab),
            pl.BlockSpec((None, D_FF, D_MODEL), slab, pipeline_mode=pl.Buffered(1)),
        ],
        out_specs=pl.BlockSpec((FFN_TM, D_MODEL), tok),
        out_shape=jax.ShapeDtypeStruct((n_tok, D_MODEL), jnp.float32),
        scratch_shapes=[
            pltpu.VMEM((CARRY_ROWS, 2 * D_FF), jnp.float32),
            pltpu.VMEM((FFN_TM, D_FF), jnp.bfloat16),
        ],
        compiler_params=pltpu.CompilerParams(
            dimension_semantics=("arbitrary",), vmem_limit_bytes=VMEM_LIMIT_BYTES),
        name="conv_glu_ffn",
    )(x2, mod, gain, wup_bf, conv_w, conv_b, wdown_bf)


def _rope_lane_tables(seq):
    half = ROPE_DIM // 2
    inv_freq = ROPE_THETA ** (-jnp.arange(0, half, dtype=jnp.float32) * 2.0 / ROPE_DIM)
    inv_lane = jnp.tile(jnp.pad(jnp.tile(inv_freq, 2), (0, HEAD_DIM - ROPE_DIM)), LANES // HEAD_DIM)
    ang_r = jnp.arange(QKV_TM, dtype=jnp.float32)[:, None] * inv_lane[None, :]
    ang_b = jnp.arange(0, seq, QKV_TM, dtype=jnp.float32)[:, None] * inv_lane[None, :]
    rot = jnp.stack([jnp.cos(ang_r), jnp.sin(ang_r)])
    base = jnp.stack([jnp.cos(ang_b), jnp.sin(ang_b)], axis=1)
    return rot, base


def kernel(x, c, mod_w, mod_b, mix_norm_gain, ffn_norm_gain, w_qkv, q_gain, k_gain, sinks, w_o,
           pool_w, pool_scale, w_up, conv_w, conv_b, w_down):
    batch, seq, d = x.shape
    assert d == D_MODEL and seq % max(QKV_TM, ATTN_TQ, FFN_TM, POOL_TM) == 0
    bf = jnp.bfloat16
    x2 = x.reshape(batch * seq, d)

    c_pad = jnp.pad(c, ((0, 8 - batch), (0, 0)))
    mod = _modulation(c_pad, mod_w, mod_b)[:, :batch]
    mod = mod.reshape(DEPTH, batch, 6, d)

    rope_rot, rope_base = _rope_lane_tables(seq)
    scale = 1.0 / math.sqrt(HEAD_DIM)
    qk_gain = jnp.concatenate(
        [jnp.tile(q_gain[0] * scale, N_Q_HEADS), jnp.tile(k_gain[0], N_KV_HEADS)])[None, :]

    q, kd, vd = _qkv_call(x2, mod[0], mix_norm_gain[0:1], w_qkv[0].astype(bf), qk_gain,
                          rope_rot, rope_base, seq)
    x2 = _attn_call(sinks[0], x2, mod[0], q, kd, vd, w_o[0].astype(bf), seq)
    ffn_args = (mod, ffn_norm_gain[:, None, :], w_up.astype(bf), conv_w, conv_b[:, None, :],
                w_down.astype(bf), seq)
    x2 = _ffn_call(0, x2, *ffn_args)
    x2 = _pool_call(x2, mod[1], mix_norm_gain[1:2], pool_w[0].astype(bf), pool_scale[0:1], seq)
    x2 = _ffn_call(1, x2, *ffn_args)
    return x2.reshape(batch, seq, d)
```

```python
import functools
import math

import jax
import jax.numpy as jnp
from jax import lax
from jax.experimental import pallas as pl
from jax.experimental.pallas import tpu as pltpu

D_MODEL = 1024
DEPTH = 2
HEAD_DIM = 64
N_Q_HEADS = 16
N_KV_HEADS = 4
GROUP = 4
Q_DIM = N_Q_HEADS * HEAD_DIM
KV_DIM = N_KV_HEADS * HEAD_DIM
QKV_DIM = Q_DIM + 2 * KV_DIM
BLOCK = 128
ROPE_DIM = HEAD_DIM // 4
ROPE_THETA = 500000.0
POOL_WINDOWS = (2, 4, 8, 16)
POOL_GROUP_DIM = D_MODEL // len(POOL_WINDOWS)
MAX_POOL_WINDOW = max(POOL_WINDOWS)
D_FF = 2816
NORM_EPS = 1e-6
NEG_INF = -1e30

LANES = 128
MXU_DIM = 256
VMEM_LIMIT_BYTES = 56 * 1024 * 1024

MOD_TN = 1024
QKV_TM = 1024
QKV_SUB = 256
ATTN_TQ = 512
FFN_TM = 1024
FFN_FC = 256
POOL_TM = 512
CONV_HEAD = 16
CARRY_ROWS = 8

_NT = (((1,), (1,)), ((), ()))


def _bdot(a, b):
    return jnp.dot(a, b, preferred_element_type=jnp.float32)


def _adaln(x, gain, scale, shift):
    ms = jnp.mean(x * x, axis=-1, keepdims=True)
    return (x * lax.rsqrt(ms + NORM_EPS)) * (gain * (1.0 + scale)) + shift


def _mod_kernel(c_ref, w_ref, b_ref, o_ref):
    c = c_ref[...]
    ca = (c * jax.nn.sigmoid(c)).astype(jnp.bfloat16)
    o_ref[...] = _bdot(ca, w_ref[...].astype(jnp.bfloat16)) + b_ref[...]


def _modulation(c_pad, mod_w, mod_b):
    rows = c_pad.shape[0]
    n = mod_w.shape[-1]
    return pl.pallas_call(
        _mod_kernel,
        grid=(DEPTH, n // MOD_TN),
        in_specs=[
            pl.BlockSpec((rows, D_MODEL), lambda l, j: (0, 0)),
            pl.BlockSpec((None, D_MODEL, MOD_TN), lambda l, j: (l, 0, j)),
            pl.BlockSpec((None, 1, MOD_TN), lambda l, j: (l, 0, j)),
        ],
        out_specs=pl.BlockSpec((None, rows, MOD_TN), lambda l, j: (l, 0, j)),
        out_shape=jax.ShapeDtypeStruct((DEPTH, rows, n), jnp.float32),
        compiler_params=pltpu.CompilerParams(
            dimension_semantics=("arbitrary", "arbitrary"),
            vmem_limit_bytes=VMEM_LIMIT_BYTES),
        name="adaln_mod",
    )(c_pad, mod_w, mod_b.reshape(DEPTH, 1, n))


def _dup_halves(col):
    lo = lax.broadcasted_iota(jnp.int32, col.shape, 1) < HEAD_DIM
    r = pltpu.roll(col, HEAD_DIM, 1)
    return jnp.where(lo, col, r), jnp.where(lo, r, col)


def _qkv_kernel(x_ref, mod_ref, gain_ref, w_ref, qkg_ref, rot_ref, base_ref,
                q_ref, kd_ref, vd_ref):
    d = lax.broadcasted_iota(jnp.int32, (1, LANES), 1) % HEAD_DIM
    first = jnp.where(d < ROPE_DIM // 2, -1.0, 0.0)
    second = jnp.where((d >= ROPE_DIM // 2) & (d < ROPE_DIM), 1.0, 0.0)
    cos_b, sin_b = base_ref[0:1, :], base_ref[1:2, :]
    r = lax.broadcasted_iota(jnp.int32, (MXU_DIM, MXU_DIM), 0) // HEAD_DIM
    cidx = lax.broadcasted_iota(jnp.int32, (MXU_DIM, MXU_DIM), 1) // HEAD_DIM
    avg = jnp.where(r == cidx, 1.0 / HEAD_DIM, 0.0).astype(jnp.bfloat16)
    qkg = qkg_ref[...]
    gain, scale, shift = gain_ref[...], mod_ref[1:2, :], mod_ref[0:1, :]
    n_chunks = (Q_DIM + KV_DIM) // MXU_DIM

    for sub in range(QKV_TM // QKV_SUB):
        rows = slice(sub * QKV_SUB, (sub + 1) * QKV_SUB)
        h = _adaln(x_ref[rows, :], gain, scale, shift).astype(jnp.bfloat16)
        qkv = _bdot(h, w_ref[...])
        cos_r, sin_r = rot_ref[0, rows, :], rot_ref[1, rows, :]
        rc = cos_b * cos_r - sin_b * sin_r
        sin = sin_b * cos_r + cos_b * sin_r
        rs1, rs2 = sin * first, sin * second
        k_cols = []
        for ci in range(n_chunks):
            t = qkv[:, ci * MXU_DIM:(ci + 1) * MXU_DIM]
            ms = _bdot((t * t).astype(jnp.bfloat16), avg)
            tn = t * lax.rsqrt(ms + NORM_EPS) * qkg[:, ci * MXU_DIM:(ci + 1) * MXU_DIM]
            for half in range(MXU_DIM // LANES):
                u = tn[:, half * LANES:(half + 1) * LANES]
                u = (u * rc + pltpu.roll(u, LANES - ROPE_DIM // 2, 1) * rs1
                     + pltpu.roll(u, ROPE_DIM // 2, 1) * rs2)
                col = ci * MXU_DIM + half * LANES
                if col < Q_DIM:
                    q_ref[rows, col:col + LANES] = u.astype(jnp.bfloat16)
                else:
                    k_cols.append(u)
        for pair, col in enumerate(k_cols):
            a, b = _dup_halves(col)
            kd_ref[2 * pair, rows, :] = a.astype(jnp.bfloat16)
            kd_ref[2 * pair + 1, rows, :] = b.astype(jnp.bfloat16)
        for pair in range(KV_DIM // LANES):
            col = qkv[:, Q_DIM + KV_DIM + pair * LANES:Q_DIM + KV_DIM + (pair + 1) * LANES]
            a, b = _dup_halves(col)
            vd_ref[2 * pair, rows, :] = a.astype(jnp.bfloat16)
            vd_ref[2 * pair + 1, rows, :] = b.astype(jnp.bfloat16)


def _qkv_call(x2, mod0, gain, w_qkv_bf, qk_gain, rope_rot, rope_base, seq):
    n_tok = x2.shape[0]
    tpb = seq // QKV_TM
    tok = lambda i: (i, 0)
    return pl.pallas_call(
        _qkv_kernel,
        grid=(n_tok // QKV_TM,),
        in_specs=[
            pl.BlockSpec((QKV_TM, D_MODEL), tok),
            pl.BlockSpec((None, 6, D_MODEL), lambda i: (i // tpb, 0, 0)),
            pl.BlockSpec((1, D_MODEL), lambda i: (0, 0)),
            pl.BlockSpec((D_MODEL, QKV_DIM), lambda i: (0, 0)),
            pl.BlockSpec((1, Q_DIM + KV_DIM), lambda i: (0, 0)),
            pl.BlockSpec((2, QKV_TM, LANES), lambda i: (0, 0, 0)),
            pl.BlockSpec((None, 2, LANES), lambda i: (i % tpb, 0, 0)),
        ],
        out_specs=[
            pl.BlockSpec((QKV_TM, Q_DIM), tok),
            pl.BlockSpec((N_KV_HEADS, QKV_TM, LANES), lambda i: (0, i, 0)),
            pl.BlockSpec((N_KV_HEADS, QKV_TM, LANES), lambda i: (0, i, 0)),
        ],
        out_shape=[
            jax.ShapeDtypeStruct((n_tok, Q_DIM), jnp.bfloat16),
            jax.ShapeDtypeStruct((N_KV_HEADS, n_tok, LANES), jnp.bfloat16),
            jax.ShapeDtypeStruct((N_KV_HEADS, n_tok, LANES), jnp.bfloat16),
        ],
        compiler_params=pltpu.CompilerParams(
            dimension_semantics=("arbitrary",), vmem_limit_bytes=VMEM_LIMIT_BYTES),
        name="qkv_norm_rope",
    )(x2, mod0, gain, w_qkv_bf, qk_gain, rope_rot, rope_base)


def _attn_kernel(sink_ref, x_ref, mod_ref, q_ref, k_ref, kp_ref, v_ref, vp_ref, wo_ref,
                 o_ref, attn_ref, *, tiles_per_seq):
    i = pl.program_id(0)
    no_prev_shift = jnp.where((i % tiles_per_seq) == 0, BLOCK, 0)
    bf = jnp.bfloat16
    rows4 = GROUP * BLOCK

    lane = lax.broadcasted_iota(jnp.int32, (BLOCK, LANES), 1)
    lo = lane < HEAD_DIM
    row_q = lax.broadcasted_iota(jnp.int32, (rows4, BLOCK), 0) % BLOCK
    col_k = lax.broadcasted_iota(jnp.int32, (rows4, BLOCK), 1)
    upper = col_k > row_q
    lane2 = lax.broadcasted_iota(jnp.int32, (2 * BLOCK, LANES), 1)
    lo2 = lane2 < HEAD_DIM
    ones_lo = jnp.where(lo2, 1.0, 0.0).astype(bf)
    ones_hi = jnp.where(lo2, 0.0, 1.0).astype(bf)
    zero_bf = jnp.zeros((), bf)

    for blk in range(ATTN_TQ // BLOCK):
        rows = slice(blk * BLOCK, (blk + 1) * BLOCK)
        for h in range(N_KV_HEADS):
            qp01 = q_ref[rows, h * MXU_DIM:h * MXU_DIM + LANES]
            qp23 = q_ref[rows, h * MXU_DIM + LANES:(h + 1) * MXU_DIM]
            qs = jnp.concatenate([
                jnp.where(lo, qp01, zero_bf), jnp.where(lo, zero_bf, qp01),
                jnp.where(lo, qp23, zero_bf), jnp.where(lo, zero_bf, qp23)], axis=0)
            if blk == 0:
                k_prev, v_prev = kp_ref[h], vp_ref[h]
            else:
                prev = slice((blk - 1) * BLOCK, blk * BLOCK)
                k_prev, v_prev = k_ref[h, prev, :], v_ref[h, prev, :]
            k_band = jnp.concatenate([k_prev, k_ref[h, rows, :]], axis=0)
            v_band = jnp.concatenate([v_prev, v_ref[h, rows, :]], axis=0)
            s2 = lax.dot_general(qs, k_band, _NT, preferred_element_type=jnp.float32)
            s_prev, s_cur = s2[:, :BLOCK], s2[:, BLOCK:]
            if blk == 0:
                use_prev = col_k > row_q + no_prev_shift
                s = jnp.where(use_prev, s_prev, jnp.where(upper, NEG_INF, s_cur))
            else:
                s = jnp.where(upper, s_prev, s_cur)
            sink = jnp.concatenate(
                [jnp.full((BLOCK, LANES), sink_ref[h * GROUP + g], jnp.float32)
                 for g in range(GROUP)], axis=0)
            m = jnp.maximum(jnp.max(s, axis=-1, keepdims=True), sink)
            p = jnp.exp(s - m).astype(bf)
            e_sink = jnp.exp(sink - m)
            p_band = jnp.concatenate(
                [jnp.where(upper, p, zero_bf), jnp.where(upper, zero_bf, p)], axis=1)
            rhs = jnp.concatenate([
                jnp.concatenate([jnp.where(lo2, v_band, zero_bf), ones_lo], axis=1),
                jnp.concatenate([jnp.where(lo2, zero_bf, v_band), ones_hi], axis=1)], axis=0)
            for pair in range(GROUP // 2):
                r0 = 2 * pair * BLOCK
                lhs = jnp.concatenate(
                    [p_band[r0:r0 + BLOCK], p_band[r0 + BLOCK:r0 + 2 * BLOCK]], axis=1)
                o = _bdot(lhs, rhs)
                es = jnp.where(lo, e_sink[r0:r0 + BLOCK], e_sink[r0 + BLOCK:r0 + 2 * BLOCK])
                out = o[:, :LANES] / (o[:, LANES:] + es)
                c0 = h * MXU_DIM + pair * LANES
                attn_ref[rows, c0:c0 + LANES] = out.astype(bf)

    y = _bdot(attn_ref[...], wo_ref[...])
    o_ref[...] = x_ref[...] + mod_ref[2:3, :] * y


def _attn_call(sinks, x2, mod0, q, kd, vd, wo_bf, seq):
    n_tok = x2.shape[0]
    tps = seq // ATTN_TQ
    per = ATTN_TQ // BLOCK
    tok = lambda i: (i, 0)
    cur = lambda i: (0, i, 0)
    prv = lambda i: (0, jnp.maximum(i * per - 1, 0), 0)
    return pl.pallas_call(
        functools.partial(_attn_kernel, tiles_per_seq=tps),
        grid=(n_tok // ATTN_TQ,),
        in_specs=[
            pl.BlockSpec(memory_space=pltpu.SMEM),
            pl.BlockSpec((ATTN_TQ, D_MODEL), tok),
            pl.BlockSpec((None, 6, D_MODEL), lambda i: (i // tps, 0, 0)),
            pl.BlockSpec((ATTN_TQ, Q_DIM), tok),
            pl.BlockSpec((N_KV_HEADS, ATTN_TQ, LANES), cur),
            pl.BlockSpec((N_KV_HEADS, BLOCK, LANES), prv),
            pl.BlockSpec((N_KV_HEADS, ATTN_TQ, LANES), cur),
            pl.BlockSpec((N_KV_HEADS, BLOCK, LANES), prv),
            pl.BlockSpec((Q_DIM, D_MODEL), lambda i: (0, 0)),
        ],
        out_specs=pl.BlockSpec((ATTN_TQ, D_MODEL), tok),
        out_shape=jax.ShapeDtypeStruct((n_tok, D_MODEL), jnp.float32),
        scratch_shapes=[pltpu.VMEM((ATTN_TQ, Q_DIM), jnp.bfloat16)],
        compiler_params=pltpu.CompilerParams(
            dimension_semantics=("arbitrary",), vmem_limit_bytes=VMEM_LIMIT_BYTES),
        name="swa_attn_out",
    )(sinks, x2, mod0, q, kd, kd, vd, vd, wo_bf)


def _pool_kernel(x_ref, mod_ref, gain_ref, pw_ref, ps_ref, o_ref, carry_ref, *, tiles_per_seq):
    i = pl.program_id(0)
    t_in_seq = i % tiles_per_seq

    @pl.when(t_in_seq == 0)
    def _():
        carry_ref[...] = jnp.zeros_like(carry_ref)

    x = x_ref[...]
    h = _adaln(x, gain_ref[...], mod_ref[1:2, :], mod_ref[0:1, :])
    tm = x.shape[0]
    pos1 = (t_in_seq * tm + 1 + lax.broadcasted_iota(jnp.int32, (tm, 1), 0)).astype(jnp.float32)
    ys = []
    for g, w in enumerate(POOL_WINDOWS):
        cols = slice(g * POOL_GROUP_DIM, (g + 1) * POOL_GROUP_DIM)
        hg = h[:, cols]
        ext = jnp.concatenate([carry_ref[:, cols], hg], axis=0)
        acc = ext
        step = 1
        while step < w:
            acc = acc + pltpu.roll(acc, step, 0)
            step *= 2
        win = acc[MAX_POOL_WINDOW:]
        d = win / jnp.minimum(pos1, float(w)) - hg
        ys.append(_bdot(d.astype(jnp.bfloat16), pw_ref[g]))
    carry_ref[...] = h[tm - MAX_POOL_WINDOW:]
    y = jnp.concatenate(ys, axis=1) * ps_ref[...]
    o_ref[...] = x + mod_ref[2:3, :] * y


def _pool_call(x2, mod1, gain, pool_w_bf, pool_scale, seq):
    n_tok = x2.shape[0]
    tps = seq // POOL_TM
    tok = lambda i: (i, 0)
    ng = len(POOL_WINDOWS)
    return pl.pallas_call(
        functools.partial(_pool_kernel, tiles_per_seq=tps),
        grid=(n_tok // POOL_TM,),
        in_specs=[
            pl.BlockSpec((POOL_TM, D_MODEL), tok),
            pl.BlockSpec((None, 6, D_MODEL), lambda i: (i // tps, 0, 0)),
            pl.BlockSpec((1, D_MODEL), lambda i: (0, 0)),
            pl.BlockSpec((ng, POOL_GROUP_DIM, POOL_GROUP_DIM), lambda i: (0, 0, 0)),
            pl.BlockSpec((1, D_MODEL), lambda i: (0, 0)),
        ],
        out_specs=pl.BlockSpec((POOL_TM, D_MODEL), tok),
        out_shape=jax.ShapeDtypeStruct((n_tok, D_MODEL), jnp.float32),
        scratch_shapes=[pltpu.VMEM((MAX_POOL_WINDOW, D_MODEL), jnp.float32)],
        compiler_params=pltpu.CompilerParams(
            dimension_semantics=("arbitrary",), vmem_limit_bytes=VMEM_LIMIT_BYTES),
        name="pool_mixer",
    )(x2, mod1, gain, pool_w_bf, pool_scale)


def _causal_conv(u, carry8, w, b):
    w0, w1, w2 = w[0:1, :], w[1:2, :], w[2:3, :]
    full = b + w2 * u + w1 * pltpu.roll(u, 1, 0) + w0 * pltpu.roll(u, 2, 0)
    uh = u[:CONV_HEAD]
    e = jnp.concatenate([carry8, uh], axis=0)
    e1 = pltpu.roll(e, 1, 0)[CARRY_ROWS:]
    e2 = pltpu.roll(e, 2, 0)[CARRY_ROWS:]
    head = b + w2 * uh + w1 * e1 + w0 * e2
    return full, head


def _ffn_kernel(x_ref, mod_ref, gain_ref, wup_ref, cw_ref, cb_ref, wdown_ref, o_ref,
                carry_ref, act_ref, *, tiles_per_seq):
    i = pl.program_id(0)

    @pl.when(i % tiles_per_seq == 0)
    def _():
        carry_ref[...] = jnp.zeros_like(carry_ref)

    x = x_ref[...]
    tm = x.shape[0]
    h = _adaln(x, gain_ref[...], mod_ref[4:5, :], mod_ref[3:4, :]).astype(jnp.bfloat16)
    for c in range(D_FF // FFN_FC):
        parts = []
        for base in (0, D_FF):
            cols = slice(base + c * FFN_FC, base + (c + 1) * FFN_FC)
            u = _bdot(h, wup_ref[:, cols])
            carry8 = carry_ref[:, cols]
            carry_ref[:, cols] = u[tm - CARRY_ROWS:]
            parts.append(_causal_conv(u, carry8, cw_ref[:, cols], cb_ref[:, cols]))
        (g_full, g_head), (v_full, v_head) = parts
        acols = slice(c * FFN_FC, (c + 1) * FFN_FC)
        act_ref[:, acols] = (g_full * jax.nn.sigmoid(g_full) * v_full).astype(jnp.bfloat16)
        act_ref[0:CONV_HEAD, acols] = (g_head * jax.nn.sigmoid(g_head) * v_head).astype(jnp.bfloat16)
    y = _bdot(act_ref[...], wdown_ref[...])
    o_ref[...] = x + mod_ref[5:6, :] * y


def _ffn_call(layer, x2, mod, gain, wup_bf, conv_w, conv_b, wdown_bf, seq):
    n_tok = x2.shape[0]
    tps = seq // FFN_TM
    tok = lambda i: (i, 0)
    slab = lambda i: (layer, 0, 0)
    return pl.pallas_call(
        functools.partial(_ffn_kernel, tiles_per_seq=tps),
        grid=(n_tok // FFN_TM,),
        in_specs=[
            pl.BlockSpec((FFN_TM, D_MODEL), tok),
            pl.BlockSpec((None, None, 6, D_MODEL), lambda i: (layer, i // tps, 0, 0)),
            pl.BlockSpec((None, 1, D_MODEL), slab),
            pl.BlockSpec((None, D_MODEL, 2 * D_FF), slab, pipeline_mode=pl.Buffered(1)),
            pl.BlockSpec((None, 3, 2 * D_FF), slab),
            pl.BlockSpec((None, 1, 2 * D_FF), slab),
            pl.BlockSpec((None, D_FF, D_MODEL), slab, pipeline_mode=pl.Buffered(1)),
        ],
        out_specs=pl.BlockSpec((FFN_TM, D_MODEL), tok),
        out_shape=jax.ShapeDtypeStruct((n_tok, D_MODEL), jnp.float32),
        scratch_shapes=[
            pltpu.VMEM((CARRY_ROWS, 2 * D_FF), jnp.float32),
            pltpu.VMEM((FFN_TM, D_FF), jnp.bfloat16),
        ],
        compiler_params=pltpu.CompilerParams(
            dimension_semantics=("arbitrary",), vmem_limit_bytes=VMEM_LIMIT_BYTES),
        name="conv_glu_ffn",
    )(x2, mod, gain, wup_bf, conv_w, conv_b, wdown_bf)


def _rope_lane_tables(seq):
    half = ROPE_DIM // 2
    inv_freq = ROPE_THETA ** (-jnp.arange(0, half, dtype=jnp.float32) * 2.0 / ROPE_DIM)
    inv_lane = jnp.tile(jnp.pad(jnp.tile(inv_freq, 2), (0, HEAD_DIM - ROPE_DIM)), LANES // HEAD_DIM)
    ang_r = jnp.arange(QKV_TM, dtype=jnp.float32)[:, None] * inv_lane[None, :]
    ang_b = jnp.arange(0, seq, QKV_TM, dtype=jnp.float32)[:, None] * inv_lane[None, :]
    rot = jnp.stack([jnp.cos(ang_r), jnp.sin(ang_r)])
    base = jnp.stack([jnp.cos(ang_b), jnp.sin(ang_b)], axis=1)
    return rot, base


def kernel(x, c, mod_w, mod_b, mix_norm_gain, ffn_norm_gain, w_qkv, q_gain, k_gain, sinks, w_o,
           pool_w, pool_scale, w_up, conv_w, conv_b, w_down):
    batch, seq, d = x.shape
    assert d == D_MODEL and seq % max(QKV_TM, ATTN_TQ, FFN_TM, POOL_TM) == 0
    bf = jnp.bfloat16
    x2 = x.reshape(batch * seq, d)

    c_pad = jnp.pad(c, ((0, 8 - batch), (0, 0)))
    mod = _modulation(c_pad, mod_w, mod_b)[:, :batch]
    mod = mod.reshape(DEPTH, batch, 6, d)

    rope_rot, rope_base = _rope_lane_tables(seq)
    scale = 1.0 / math.sqrt(HEAD_DIM)
    qk_gain = jnp.concatenate(
        [jnp.tile(q_gain[0] * scale, N_Q_HEADS), jnp.tile(k_gain[0], N_KV_HEADS)])[None, :]

    q, kd, vd = _qkv_call(x2, mod[0], mix_norm_gain[0:1], w_qkv[0].astype(bf), qk_gain,
                          rope_rot, rope_base, seq)
    x2 = _attn_call(sinks[0], x2, mod[0], q, kd, vd, w_o[0].astype(bf), seq)
    ffn_args = (mod, ffn_norm_gain[:, None, :], w_up.astype(bf), conv_w, conv_b[:, None, :],
                w_down.astype(bf), seq)
    x2 = _ffn_call(0, x2, *ffn_args)
    x2 = _pool_call(x2, mod[1], mix_norm_gain[1:2], pool_w[0].astype(bf), pool_scale[0:1], seq)
    x2 = _ffn_call(1, x2, *ffn_args)
    return x2.reshape(batch, seq, d)
```

```python
import functools
import math

import jax
import jax.numpy as jnp
from jax import lax
from jax.experimental import pallas as pl
from jax.experimental.pallas import tpu as pltpu

D_MODEL = 1024
DEPTH = 2
HEAD_DIM = 64
N_Q_HEADS = 16
N_KV_HEADS = 4
GROUP = 4
Q_DIM = N_Q_HEADS * HEAD_DIM
KV_DIM = N_KV_HEADS * HEAD_DIM
QKV_DIM = Q_DIM + 2 * KV_DIM
BLOCK = 128
ROPE_DIM = HEAD_DIM // 4
ROPE_THETA = 500000.0
POOL_WINDOWS = (2, 4, 8, 16)
POOL_GROUP_DIM = D_MODEL // len(POOL_WINDOWS)
MAX_POOL_WINDOW = max(POOL_WINDOWS)
D_FF = 2816
NORM_EPS = 1e-6
NEG_INF = -1e30

LANES = 128
MXU_DIM = 256
VMEM_LIMIT_BYTES = 56 * 1024 * 1024

MOD_TN = 1024
QKV_TM = 1024
QKV_SUB = 256
ATTN_TQ = 512
FFN_TM = 512
FFN_FC = 256
CONV_HEAD = 16
CARRY_ROWS = 8

_NT = (((1,), (1,)), ((), ()))


def _bdot(a, b):
    return jnp.dot(a, b, preferred_element_type=jnp.float32)


def _adaln(x, gain, scale, shift):
    ms = jnp.mean(x * x, axis=-1, keepdims=True)
    return (x * lax.rsqrt(ms + NORM_EPS)) * (gain * (1.0 + scale)) + shift


def _mod_kernel(c_ref, w_ref, b_ref, o_ref):
    c = c_ref[...]
    ca = (c * jax.nn.sigmoid(c)).astype(jnp.bfloat16)
    o_ref[...] = _bdot(ca, w_ref[...].astype(jnp.bfloat16)) + b_ref[...]


def _modulation(c_pad, mod_w, mod_b):
    rows = c_pad.shape[0]
    n = mod_w.shape[-1]
    return pl.pallas_call(
        _mod_kernel,
        grid=(DEPTH, n // MOD_TN),
        in_specs=[
            pl.BlockSpec((rows, D_MODEL), lambda l, j: (0, 0)),
            pl.BlockSpec((None, D_MODEL, MOD_TN), lambda l, j: (l, 0, j)),
            pl.BlockSpec((None, 1, MOD_TN), lambda l, j: (l, 0, j)),
        ],
        out_specs=pl.BlockSpec((None, rows, MOD_TN), lambda l, j: (l, 0, j)),
        out_shape=jax.ShapeDtypeStruct((DEPTH, rows, n), jnp.float32),
        compiler_params=pltpu.CompilerParams(
            dimension_semantics=("arbitrary", "arbitrary"),
            vmem_limit_bytes=VMEM_LIMIT_BYTES),
        name="adaln_mod",
    )(c_pad, mod_w, mod_b.reshape(DEPTH, 1, n))


def _dup_halves(col):
    lo = lax.broadcasted_iota(jnp.int32, col.shape, 1) < HEAD_DIM
    r = pltpu.roll(col, HEAD_DIM, 1)
    return jnp.where(lo, col, r), jnp.where(lo, r, col)


def _qkv_kernel(x_ref, mod_ref, gain_ref, w_ref, qkg_ref, rot_ref, base_ref,
                q_ref, kd_ref, vd_ref):
    d = lax.broadcasted_iota(jnp.int32, (1, LANES), 1) % HEAD_DIM
    first = jnp.where(d < ROPE_DIM // 2, -1.0, 0.0)
    second = jnp.where((d >= ROPE_DIM // 2) & (d < ROPE_DIM), 1.0, 0.0)
    cos_b, sin_b = base_ref[0:1, :], base_ref[1:2, :]
    r = lax.broadcasted_iota(jnp.int32, (MXU_DIM, MXU_DIM), 0) // HEAD_DIM
    cidx = lax.broadcasted_iota(jnp.int32, (MXU_DIM, MXU_DIM), 1) // HEAD_DIM
    avg = jnp.where(r == cidx, 1.0 / HEAD_DIM, 0.0).astype(jnp.bfloat16)
    qkg = qkg_ref[...]
    gain, scale, shift = gain_ref[...], mod_ref[1:2, :], mod_ref[0:1, :]
    n_chunks = (Q_DIM + KV_DIM) // MXU_DIM

    for sub in range(QKV_TM // QKV_SUB):
        rows = slice(sub * QKV_SUB, (sub + 1) * QKV_SUB)
        h = _adaln(x_ref[rows, :], gain, scale, shift).astype(jnp.bfloat16)
        qkv = _bdot(h, w_ref[...])
        cos_r, sin_r = rot_ref[0, rows, :], rot_ref[1, rows, :]
        rc = cos_b * cos_r - sin_b * sin_r
        sin = sin_b * cos_r + cos_b * sin_r
        rs1, rs2 = sin * first, sin * second
        k_cols = []
        for ci in range(n_chunks):
            t = qkv[:, ci * MXU_DIM:(ci + 1) * MXU_DIM]
            ms = _bdot((t * t).astype(jnp.bfloat16), avg)
            tn = t * lax.rsqrt(ms + NORM_EPS) * qkg[:, ci * MXU_DIM:(ci + 1) * MXU_DIM]
            for half in range(MXU_DIM // LANES):
                u = tn[:, half * LANES:(half + 1) * LANES]
                u = (u * rc + pltpu.roll(u, LANES - ROPE_DIM // 2, 1) * rs1
                     + pltpu.roll(u, ROPE_DIM // 2, 1) * rs2)
                col = ci * MXU_DIM + half * LANES
                if col < Q_DIM:
                    q_ref[rows, col:col + LANES] = u.astype(jnp.bfloat16)
                else:
                    k_cols.append(u)
        for pair, col in enumerate(k_cols):
            a, b = _dup_halves(col)
            kd_ref[2 * pair, rows, :] = a.astype(jnp.bfloat16)
            kd_ref[2 * pair + 1, rows, :] = b.astype(jnp.bfloat16)
        for pair in range(KV_DIM // LANES):
            col = qkv[:, Q_DIM + KV_DIM + pair * LANES:Q_DIM + KV_DIM + (pair + 1) * LANES]
            a, b = _dup_halves(col)
            vd_ref[2 * pair, rows, :] = a.astype(jnp.bfloat16)
            vd_ref[2 * pair + 1, rows, :] = b.astype(jnp.bfloat16)


def _qkv_call(x2, mod0, gain, w_qkv_bf, qk_gain, rope_rot, rope_base, seq):
    n_tok = x2.shape[0]
    tpb = seq // QKV_TM
    tok = lambda i: (i, 0)
    return pl.pallas_call(
        _qkv_kernel,
        grid=(n_tok // QKV_TM,),
        in_specs=[
            pl.BlockSpec((QKV_TM, D_MODEL), tok),
            pl.BlockSpec((None, 6, D_MODEL), lambda i: (i // tpb, 0, 0)),
            pl.BlockSpec((1, D_MODEL), lambda i: (0, 0)),
            pl.BlockSpec((D_MODEL, QKV_DIM), lambda i: (0, 0)),
            pl.BlockSpec((1, Q_DIM + KV_DIM), lambda i: (0, 0)),
            pl.BlockSpec((2, QKV_TM, LANES), lambda i: (0, 0, 0)),
            pl.BlockSpec((None, 2, LANES), lambda i: (i % tpb, 0, 0)),
        ],
        out_specs=[
            pl.BlockSpec((QKV_TM, Q_DIM), tok),
            pl.BlockSpec((N_KV_HEADS, QKV_TM, LANES), lambda i: (0, i, 0)),
            pl.BlockSpec((N_KV_HEADS, QKV_TM, LANES), lambda i: (0, i, 0)),
        ],
        out_shape=[
            jax.ShapeDtypeStruct((n_tok, Q_DIM), jnp.bfloat16),
            jax.ShapeDtypeStruct((N_KV_HEADS, n_tok, LANES), jnp.bfloat16),
            jax.ShapeDtypeStruct((N_KV_HEADS, n_tok, LANES), jnp.bfloat16),
        ],
        compiler_params=pltpu.CompilerParams(
            dimension_semantics=("arbitrary",), vmem_limit_bytes=VMEM_LIMIT_BYTES),
        name="qkv_norm_rope",
    )(x2, mod0, gain, w_qkv_bf, qk_gain, rope_rot, rope_base)


def _attn_kernel(sink_ref, x_ref, mod_ref, q_ref, k_ref, kp_ref, v_ref, vp_ref, wo_ref,
                 o_ref, attn_ref, *, tiles_per_seq):
    i = pl.program_id(0)
    no_prev_shift = jnp.where((i % tiles_per_seq) == 0, BLOCK, 0)
    bf = jnp.bfloat16
    rows4 = GROUP * BLOCK

    lane = lax.broadcasted_iota(jnp.int32, (BLOCK, LANES), 1)
    lo = lane < HEAD_DIM
    row_q = lax.broadcasted_iota(jnp.int32, (rows4, BLOCK), 0) % BLOCK
    col_k = lax.broadcasted_iota(jnp.int32, (rows4, BLOCK), 1)
    upper = col_k > row_q
    lane2 = lax.broadcasted_iota(jnp.int32, (2 * BLOCK, LANES), 1)
    lo2 = lane2 < HEAD_DIM
    ones_lo = jnp.where(lo2, 1.0, 0.0).astype(bf)
    ones_hi = jnp.where(lo2, 0.0, 1.0).astype(bf)
    zero_bf = jnp.zeros((), bf)

    for blk in range(ATTN_TQ // BLOCK):
        rows = slice(blk * BLOCK, (blk + 1) * BLOCK)
        for h in range(N_KV_HEADS):
            qp01 = q_ref[rows, h * MXU_DIM:h * MXU_DIM + LANES]
            qp23 = q_ref[rows, h * MXU_DIM + LANES:(h + 1) * MXU_DIM]
            qs = jnp.concatenate([
                jnp.where(lo, qp01, zero_bf), jnp.where(lo, zero_bf, qp01),
                jnp.where(lo, qp23, zero_bf), jnp.where(lo, zero_bf, qp23)], axis=0)
            if blk == 0:
                k_prev, v_prev = kp_ref[h], vp_ref[h]
            else:
                prev = slice((blk - 1) * BLOCK, blk * BLOCK)
                k_prev, v_prev = k_ref[h, prev, :], v_ref[h, prev, :]
            k_band = jnp.concatenate([k_prev, k_ref[h, rows, :]], axis=0)
            v_band = jnp.concatenate([v_prev, v_ref[h, rows, :]], axis=0)
            s2 = lax.dot_general(qs, k_band, _NT, preferred_element_type=jnp.float32)
            s_prev, s_cur = s2[:, :BLOCK], s2[:, BLOCK:]
            if blk == 0:
                use_prev = col_k > row_q + no_prev_shift
                s = jnp.where(use_prev, s_prev, jnp.where(upper, NEG_INF, s_cur))
            else:
                s = jnp.where(upper, s_prev, s_cur)
            sink = jnp.concatenate(
                [jnp.full((BLOCK, LANES), sink_ref[h * GROUP + g], jnp.float32)
                 for g in range(GROUP)], axis=0)
            m = jnp.maximum(jnp.max(s, axis=-1, keepdims=True), sink)
            p = jnp.exp(s - m).astype(bf)
            e_sink = jnp.exp(sink - m)
            p_band = jnp.concatenate(
                [jnp.where(upper, p, zero_bf), jnp.where(upper, zero_bf, p)], axis=1)
            rhs = jnp.concatenate([
                jnp.concatenate([jnp.where(lo2, v_band, zero_bf), ones_lo], axis=1),
                jnp.concatenate([jnp.where(lo2, zero_bf, v_band), ones_hi], axis=1)], axis=0)
            for pair in range(GROUP // 2):
                r0 = 2 * pair * BLOCK
                lhs = jnp.concatenate(
                    [p_band[r0:r0 + BLOCK], p_band[r0 + BLOCK:r0 + 2 * BLOCK]], axis=1)
                o = _bdot(lhs, rhs)
                es = jnp.where(lo, e_sink[r0:r0 + BLOCK], e_sink[r0 + BLOCK:r0 + 2 * BLOCK])
                out = o[:, :LANES] / (o[:, LANES:] + es)
                c0 = h * MXU_DIM + pair * LANES
                attn_ref[rows, c0:c0 + LANES] = out.astype(bf)

    y = _bdot(attn_ref[...], wo_ref[...])
    o_ref[...] = x_ref[...] + mod_ref[2:3, :] * y


def _attn_call(sinks, x2, mod0, q, kd, vd, wo_bf, seq):
    n_tok = x2.shape[0]
    tps = seq // ATTN_TQ
    per = ATTN_TQ // BLOCK
    tok = lambda i: (i, 0)
    cur = lambda i: (0, i, 0)
    prv = lambda i: (0, jnp.maximum(i * per - 1, 0), 0)
    return pl.pallas_call(
        functools.partial(_attn_kernel, tiles_per_seq=tps),
        grid=(n_tok // ATTN_TQ,),
        in_specs=[
            pl.BlockSpec(memory_space=pltpu.SMEM),
            pl.BlockSpec((ATTN_TQ, D_MODEL), tok),
            pl.BlockSpec((None, 6, D_MODEL), lambda i: (i // tps, 0, 0)),
            pl.BlockSpec((ATTN_TQ, Q_DIM), tok),
            pl.BlockSpec((N_KV_HEADS, ATTN_TQ, LANES), cur),
            pl.BlockSpec((N_KV_HEADS, BLOCK, LANES), prv),
            pl.BlockSpec((N_KV_HEADS, ATTN_TQ, LANES), cur),
            pl.BlockSpec((N_KV_HEADS, BLOCK, LANES), prv),
            pl.BlockSpec((Q_DIM, D_MODEL), lambda i: (0, 0)),
        ],
        out_specs=pl.BlockSpec((ATTN_TQ, D_MODEL), tok),
        out_shape=jax.ShapeDtypeStruct((n_tok, D_MODEL), jnp.float32),
        scratch_shapes=[pltpu.VMEM((ATTN_TQ, Q_DIM), jnp.bfloat16)],
        compiler_params=pltpu.CompilerParams(
            dimension_semantics=("arbitrary",), vmem_limit_bytes=VMEM_LIMIT_BYTES),
        name="swa_attn_out",
    )(sinks, x2, mod0, q, kd, kd, vd, vd, wo_bf)


def _pool_mix(x, t_in_seq, mod_ref, gain_ref, pw_ref, ps_ref, carry_ref):
    h = _adaln(x, gain_ref[...], mod_ref[1:2, :], mod_ref[0:1, :])
    tm = x.shape[0]
    pos1 = (t_in_seq * tm + 1 + lax.broadcasted_iota(jnp.int32, (tm, 1), 0)).astype(jnp.float32)
    ys = []
    for g, w in enumerate(POOL_WINDOWS):
        cols = slice(g * POOL_GROUP_DIM, (g + 1) * POOL_GROUP_DIM)
        hg = h[:, cols]
        ext = jnp.concatenate([carry_ref[:, cols], hg], axis=0)
        acc = ext
        step = 1
        while step < w:
            acc = acc + pltpu.roll(acc, step, 0)
            step *= 2
        win = acc[MAX_POOL_WINDOW:]
        d = win / jnp.minimum(pos1, float(w)) - hg
        ys.append(_bdot(d.astype(jnp.bfloat16), pw_ref[g]))
    carry_ref[...] = h[tm - MAX_POOL_WINDOW:]
    y = jnp.concatenate(ys, axis=1) * ps_ref[...]
    return x + mod_ref[2:3, :] * y


def _causal_conv(u, carry8, w, b):
    w0, w1, w2 = w[0:1, :], w[1:2, :], w[2:3, :]
    full = b + w2 * u + w1 * pltpu.roll(u, 1, 0) + w0 * pltpu.roll(u, 2, 0)
    uh = u[:CONV_HEAD]
    e = jnp.concatenate([carry8, uh], axis=0)
    e1 = pltpu.roll(e, 1, 0)[CARRY_ROWS:]
    e2 = pltpu.roll(e, 2, 0)[CARRY_ROWS:]
    head = b + w2 * uh + w1 * e1 + w0 * e2
    return full, head


def _ffn_kernel(*refs, tiles_per_seq, pool_first):
    if pool_first:
        (x_ref, mod_ref, gain_ref, wup_ref, cw_ref, cb_ref, wdown_ref, pgain_ref, pw_ref, ps_ref,
         o_ref, carry_ref, act_ref, pcarry_ref) = refs
    else:
        x_ref, mod_ref, gain_ref, wup_ref, cw_ref, cb_ref, wdown_ref, o_ref, carry_ref, act_ref = refs
    i = pl.program_id(0)
    t_in_seq = i % tiles_per_seq

    @pl.when(t_in_seq == 0)
    def _():
        carry_ref[...] = jnp.zeros_like(carry_ref)
        if pool_first:
            pcarry_ref[...] = jnp.zeros_like(pcarry_ref)

    x = x_ref[...]
    if pool_first:
        x = _pool_mix(x, t_in_seq, mod_ref, pgain_ref, pw_ref, ps_ref, pcarry_ref)
    tm = x.shape[0]
    h = _adaln(x, gain_ref[...], mod_ref[4:5, :], mod_ref[3:4, :]).astype(jnp.bfloat16)
    for c in range(D_FF // FFN_FC):
        parts = []
        for base in (0, D_FF):
            cols = slice(base + c * FFN_FC, base + (c + 1) * FFN_FC)
            u = _bdot(h, wup_ref[:, cols])
            carry8 = carry_ref[:, cols]
            carry_ref[:, cols] = u[tm - CARRY_ROWS:]
            parts.append(_causal_conv(u, carry8, cw_ref[:, cols], cb_ref[:, cols]))
        (g_full, g_head), (v_full, v_head) = parts
        acols = slice(c * FFN_FC, (c + 1) * FFN_FC)
        act_ref[:, acols] = (g_full * jax.nn.sigmoid(g_full) * v_full).astype(jnp.bfloat16)
        act_ref[0:CONV_HEAD, acols] = (g_head * jax.nn.sigmoid(g_head) * v_head).astype(jnp.bfloat16)
    y = _bdot(act_ref[...], wdown_ref[...])
    o_ref[...] = x + mod_ref[5:6, :] * y


def _ffn_call(layer, x2, mod, gain, wup_bf, conv_w, conv_b, wdown_bf, seq, pool=None):
    n_tok = x2.shape[0]
    tps = seq // FFN_TM
    tok = lambda i: (i, 0)
    slab = lambda i: (layer, 0, 0)
    whole2 = lambda i: (0, 0)
    in_specs = [
        pl.BlockSpec((FFN_TM, D_MODEL), tok),
        pl.BlockSpec((None, None, 6, D_MODEL), lambda i: (layer, i // tps, 0, 0)),
        pl.BlockSpec((None, 1, D_MODEL), slab),
        pl.BlockSpec((None, D_MODEL, 2 * D_FF), slab, pipeline_mode=pl.Buffered(1)),
        pl.BlockSpec((None, 3, 2 * D_FF), slab),
        pl.BlockSpec((None, 1, 2 * D_FF), slab),
        pl.BlockSpec((None, D_FF, D_MODEL), slab, pipeline_mode=pl.Buffered(1)),
    ]
    scratch = [
        pltpu.VMEM((CARRY_ROWS, 2 * D_FF), jnp.float32),
        pltpu.VMEM((FFN_TM, D_FF), jnp.bfloat16),
    ]
    args = [x2, mod, gain, wup_bf, conv_w, conv_b, wdown_bf]
    if pool is not None:
        ng = len(POOL_WINDOWS)
        in_specs += [
            pl.BlockSpec((1, D_MODEL), whole2),
            pl.BlockSpec((ng, POOL_GROUP_DIM, POOL_GROUP_DIM), lambda i: (0, 0, 0)),
            pl.BlockSpec((1, D_MODEL), whole2),
        ]
        scratch.append(pltpu.VMEM((MAX_POOL_WINDOW, D_MODEL), jnp.float32))
        args += list(pool)
    return pl.pallas_call(
        functools.partial(_ffn_kernel, tiles_per_seq=tps, pool_first=pool is not None),
        grid=(n_tok // FFN_TM,),
        in_specs=in_specs,
        out_specs=pl.BlockSpec((FFN_TM, D_MODEL), tok),
        out_shape=jax.ShapeDtypeStruct((n_tok, D_MODEL), jnp.float32),
        scratch_shapes=scratch,
        compiler_params=pltpu.CompilerParams(
            dimension_semantics=("arbitrary",), vmem_limit_bytes=VMEM_LIMIT_BYTES),
        name="pool_conv_glu_ffn" if pool is not None else "conv_glu_ffn",
    )(*args)


def _rope_lane_tables(seq):
    half = ROPE_DIM // 2
    inv_freq = ROPE_THETA ** (-jnp.arange(0, half, dtype=jnp.float32) * 2.0 / ROPE_DIM)
    inv_lane = jnp.tile(jnp.pad(jnp.tile(inv_freq, 2), (0, HEAD_DIM - ROPE_DIM)), LANES // HEAD_DIM)
    ang_r = jnp.arange(QKV_TM, dtype=jnp.float32)[:, None] * inv_lane[None, :]
    ang_b = jnp.arange(0, seq, QKV_TM, dtype=jnp.float32)[:, None] * inv_lane[None, :]
    rot = jnp.stack([jnp.cos(ang_r), jnp.sin(ang_r)])
    base = jnp.stack([jnp.cos(ang_b), jnp.sin(ang_b)], axis=1)
    return rot, base


def kernel(x, c, mod_w, mod_b, mix_norm_gain, ffn_norm_gain, w_qkv, q_gain, k_gain, sinks, w_o,
           pool_w, pool_scale, w_up, conv_w, conv_b, w_down):
    batch, seq, d = x.shape
    assert d == D_MODEL and seq % max(QKV_TM, ATTN_TQ, FFN_TM) == 0
    bf = jnp.bfloat16
    x2 = x.reshape(batch * seq, d)

    c_pad = jnp.pad(c, ((0, 8 - batch), (0, 0)))
    mod = _modulation(c_pad, mod_w, mod_b)[:, :batch]
    mod = mod.reshape(DEPTH, batch, 6, d)

    rope_rot, rope_base = _rope_lane_tables(seq)
    scale = 1.0 / math.sqrt(HEAD_DIM)
    qk_gain = jnp.concatenate(
        [jnp.tile(q_gain[0] * scale, N_Q_HEADS), jnp.tile(k_gain[0], N_KV_HEADS)])[None, :]

    q, kd, vd = _qkv_call(x2, mod[0], mix_norm_gain[0:1], w_qkv[0].astype(bf), qk_gain,
                          rope_rot, rope_base, seq)
    x2 = _attn_call(sinks[0], x2, mod[0], q, kd, vd, w_o[0].astype(bf), seq)
    ffn_args = (mod, ffn_norm_gain[:, None, :], w_up.astype(bf), conv_w, conv_b[:, None, :],
                w_down.astype(bf), seq)
    x2 = _ffn_call(0, x2, *ffn_args)
    x2 = _ffn_call(1, x2, *ffn_args,
                   pool=(mix_norm_gain[1:2], pool_w[0].astype(bf), pool_scale[0:1]))
    return x2.reshape(batch, seq, d)
```

```python
import functools
import math

import jax
import jax.numpy as jnp
from jax import lax
from jax.experimental import pallas as pl
from jax.experimental.pallas import tpu as pltpu

D_MODEL = 1024
DEPTH = 2
HEAD_DIM = 64
N_Q_HEADS = 16
N_KV_HEADS = 4
GROUP = 4
Q_DIM = N_Q_HEADS * HEAD_DIM
KV_DIM = N_KV_HEADS * HEAD_DIM
QKV_DIM = Q_DIM + 2 * KV_DIM
BLOCK = 128
ROPE_DIM = HEAD_DIM // 4
ROPE_THETA = 500000.0
POOL_WINDOWS = (2, 4, 8, 16)
POOL_GROUP_DIM = D_MODEL // len(POOL_WINDOWS)
MAX_POOL_WINDOW = max(POOL_WINDOWS)
D_FF = 2816
NORM_EPS = 1e-6
NEG_INF = -1e30

LANES = 128
MXU_DIM = 256
VMEM_LIMIT_BYTES = 56 * 1024 * 1024

MOD_TN = 1024
QKV_TM = 1024
QKV_SUB = 256
ATTN_TQ = 512
FFN_TM = 512
FFN_FC = 256
CONV_HEAD = 16
CARRY_ROWS = 8

_NT = (((1,), (1,)), ((), ()))


def _bdot(a, b):
    return jnp.dot(a, b, preferred_element_type=jnp.float32)


def _adaln(x, gain, scale, shift):
    ms = jnp.mean(x * x, axis=-1, keepdims=True)
    return (x * lax.rsqrt(ms + NORM_EPS)) * (gain * (1.0 + scale)) + shift


def _mod_kernel(c_ref, w_ref, b_ref, o_ref):
    c = c_ref[...]
    ca = (c * jax.nn.sigmoid(c)).astype(jnp.bfloat16)
    o_ref[...] = _bdot(ca, w_ref[...].astype(jnp.bfloat16)) + b_ref[...]


def _modulation(c_pad, mod_w, mod_b):
    rows = c_pad.shape[0]
    n = mod_w.shape[-1]
    return pl.pallas_call(
        _mod_kernel,
        grid=(DEPTH, n // MOD_TN),
        in_specs=[
            pl.BlockSpec((rows, D_MODEL), lambda l, j: (0, 0)),
            pl.BlockSpec((None, D_MODEL, MOD_TN), lambda l, j: (l, 0, j)),
            pl.BlockSpec((None, 1, MOD_TN), lambda l, j: (l, 0, j)),
        ],
        out_specs=pl.BlockSpec((None, rows, MOD_TN), lambda l, j: (l, 0, j)),
        out_shape=jax.ShapeDtypeStruct((DEPTH, rows, n), jnp.float32),
        compiler_params=pltpu.CompilerParams(
            dimension_semantics=("arbitrary", "arbitrary"),
            vmem_limit_bytes=VMEM_LIMIT_BYTES),
        name="adaln_mod",
    )(c_pad, mod_w, mod_b.reshape(DEPTH, 1, n))


def _dup_halves(col):
    lo = lax.broadcasted_iota(jnp.int32, col.shape, 1) < HEAD_DIM
    r = pltpu.roll(col, HEAD_DIM, 1)
    return jnp.where(lo, col, r), jnp.where(lo, r, col)


def _qkv_kernel(*refs, n_cast):
    x_ref, mod_ref, gain_ref, w32_ref, qkg_ref, rot_ref, base_ref = refs[:7]
    cast_src = refs[7:7 + n_cast]
    q_ref, kd_ref, vd_ref = refs[7 + n_cast:10 + n_cast]
    cast_dst = refs[10 + n_cast:10 + 2 * n_cast]
    w_ref = refs[10 + 2 * n_cast]

    @pl.when(pl.program_id(0) == 0)
    def _():
        w_ref[...] = w32_ref[...].astype(jnp.bfloat16)

    for src, dst in zip(cast_src, cast_dst):
        dst[...] = src[...].astype(jnp.bfloat16)

    d = lax.broadcasted_iota(jnp.int32, (1, LANES), 1) % HEAD_DIM
    first = jnp.where(d < ROPE_DIM // 2, -1.0, 0.0)
    second = jnp.where((d >= ROPE_DIM // 2) & (d < ROPE_DIM), 1.0, 0.0)
    cos_b, sin_b = base_ref[0:1, :], base_ref[1:2, :]
    r = lax.broadcasted_iota(jnp.int32, (MXU_DIM, MXU_DIM), 0) // HEAD_DIM
    cidx = lax.broadcasted_iota(jnp.int32, (MXU_DIM, MXU_DIM), 1) // HEAD_DIM
    avg = jnp.where(r == cidx, 1.0 / HEAD_DIM, 0.0).astype(jnp.bfloat16)
    qkg = qkg_ref[...]
    gain, scale, shift = gain_ref[...], mod_ref[1:2, :], mod_ref[0:1, :]
    n_chunks = (Q_DIM + KV_DIM) // MXU_DIM

    for sub in range(QKV_TM // QKV_SUB):
        rows = slice(sub * QKV_SUB, (sub + 1) * QKV_SUB)
        h = _adaln(x_ref[rows, :], gain, scale, shift).astype(jnp.bfloat16)
        qkv = _bdot(h, w_ref[...])
        cos_r, sin_r = rot_ref[0, rows, :], rot_ref[1, rows, :]
        rc = cos_b * cos_r - sin_b * sin_r
        sin = sin_b * cos_r + cos_b * sin_r
        rs1, rs2 = sin * first, sin * second
        k_cols = []
        for ci in range(n_chunks):
            t = qkv[:, ci * MXU_DIM:(ci + 1) * MXU_DIM]
            ms = _bdot((t * t).astype(jnp.bfloat16), avg)
            tn = t * lax.rsqrt(ms + NORM_EPS) * qkg[:, ci * MXU_DIM:(ci + 1) * MXU_DIM]
            for half in range(MXU_DIM // LANES):
                u = tn[:, half * LANES:(half + 1) * LANES]
                u = (u * rc + pltpu.roll(u, LANES - ROPE_DIM // 2, 1) * rs1
                     + pltpu.roll(u, ROPE_DIM // 2, 1) * rs2)
                col = ci * MXU_DIM + half * LANES
                if col < Q_DIM:
                    q_ref[rows, col:col + LANES] = u.astype(jnp.bfloat16)
                else:
                    k_cols.append(u)
        for pair, col in enumerate(k_cols):
            a, b = _dup_halves(col)
            kd_ref[2 * pair, rows, :] = a.astype(jnp.bfloat16)
            kd_ref[2 * pair + 1, rows, :] = b.astype(jnp.bfloat16)
        for pair in range(KV_DIM // LANES):
            col = qkv[:, Q_DIM + KV_DIM + pair * LANES:Q_DIM + KV_DIM + (pair + 1) * LANES]
            a, b = _dup_halves(col)
            vd_ref[2 * pair, rows, :] = a.astype(jnp.bfloat16)
            vd_ref[2 * pair + 1, rows, :] = b.astype(jnp.bfloat16)


def _qkv_call(x2, mod0, gain, w_qkv, qk_gain, rope_rot, rope_base, later_weights, seq):
    n_tok = x2.shape[0]
    n_steps = n_tok // QKV_TM
    tpb = seq // QKV_TM
    tok = lambda i: (i, 0)
    bf16_rows = 2 * 8
    cast_specs, cast_shapes = [], []
    for w in later_weights:
        slab = w.shape[0] // n_steps
        assert w.ndim == 2 and slab * n_steps == w.shape[0] and slab % bf16_rows == 0, w.shape
        cast_specs.append(pl.BlockSpec((slab, w.shape[1]), tok))
        cast_shapes.append(jax.ShapeDtypeStruct(w.shape, jnp.bfloat16))
    outs = pl.pallas_call(
        functools.partial(_qkv_kernel, n_cast=len(later_weights)),
        grid=(n_steps,),
        in_specs=[
            pl.BlockSpec((QKV_TM, D_MODEL), tok),
            pl.BlockSpec((None, 6, D_MODEL), lambda i: (i // tpb, 0, 0)),
            pl.BlockSpec((1, D_MODEL), lambda i: (0, 0)),
            pl.BlockSpec((D_MODEL, QKV_DIM), lambda i: (0, 0), pipeline_mode=pl.Buffered(1)),
            pl.BlockSpec((1, Q_DIM + KV_DIM), lambda i: (0, 0)),
            pl.BlockSpec((2, QKV_TM, LANES), lambda i: (0, 0, 0)),
            pl.BlockSpec((None, 2, LANES), lambda i: (i % tpb, 0, 0)),
        ] + cast_specs,
        out_specs=[
            pl.BlockSpec((QKV_TM, Q_DIM), tok),
            pl.BlockSpec((N_KV_HEADS, QKV_TM, LANES), lambda i: (0, i, 0)),
            pl.BlockSpec((N_KV_HEADS, QKV_TM, LANES), lambda i: (0, i, 0)),
        ] + cast_specs,
        out_shape=[
            jax.ShapeDtypeStruct((n_tok, Q_DIM), jnp.bfloat16),
            jax.ShapeDtypeStruct((N_KV_HEADS, n_tok, LANES), jnp.bfloat16),
            jax.ShapeDtypeStruct((N_KV_HEADS, n_tok, LANES), jnp.bfloat16),
        ] + cast_shapes,
        scratch_shapes=[pltpu.VMEM((D_MODEL, QKV_DIM), jnp.bfloat16)],
        compiler_params=pltpu.CompilerParams(
            dimension_semantics=("arbitrary",), vmem_limit_bytes=VMEM_LIMIT_BYTES),
        name="qkv_norm_rope",
    )(x2, mod0, gain, w_qkv, qk_gain, rope_rot, rope_base, *later_weights)
    return outs[:3], outs[3:]


def _attn_kernel(sink_ref, x_ref, mod_ref, q_ref, k_ref, kp_ref, v_ref, vp_ref, wo_ref,
                 o_ref, attn_ref, *, tiles_per_seq):
    i = pl.program_id(0)
    no_prev_shift = jnp.where((i % tiles_per_seq) == 0, BLOCK, 0)
    bf = jnp.bfloat16
    rows4 = GROUP * BLOCK

    lane = lax.broadcasted_iota(jnp.int32, (BLOCK, LANES), 1)
    lo = lane < HEAD_DIM
    row_q = lax.broadcasted_iota(jnp.int32, (rows4, BLOCK), 0) % BLOCK
    col_k = lax.broadcasted_iota(jnp.int32, (rows4, BLOCK), 1)
    upper = col_k > row_q
    lane2 = lax.broadcasted_iota(jnp.int32, (2 * BLOCK, LANES), 1)
    lo2 = lane2 < HEAD_DIM
    ones_lo = jnp.where(lo2, 1.0, 0.0).astype(bf)
    ones_hi = jnp.where(lo2, 0.0, 1.0).astype(bf)
    zero_bf = jnp.zeros((), bf)

    for blk in range(ATTN_TQ // BLOCK):
        rows = slice(blk * BLOCK, (blk + 1) * BLOCK)
        for h in range(N_KV_HEADS):
            qp01 = q_ref[rows, h * MXU_DIM:h * MXU_DIM + LANES]
            qp23 = q_ref[rows, h * MXU_DIM + LANES:(h + 1) * MXU_DIM]
            qs = jnp.concatenate([
                jnp.where(lo, qp01, zero_bf), jnp.where(lo, zero_bf, qp01),
                jnp.where(lo, qp23, zero_bf), jnp.where(lo, zero_bf, qp23)], axis=0)
            if blk == 0:
                k_prev, v_prev = kp_ref[h], vp_ref[h]
            else:
                prev = slice((blk - 1) * BLOCK, blk * BLOCK)
                k_prev, v_prev = k_ref[h, prev, :], v_ref[h, prev, :]
            k_band = jnp.concatenate([k_prev, k_ref[h, rows, :]], axis=0)
            v_band = jnp.concatenate([v_prev, v_ref[h, rows, :]], axis=0)
            s2 = lax.dot_general(qs, k_band, _NT, preferred_element_type=jnp.float32)
            s_prev, s_cur = s2[:, :BLOCK], s2[:, BLOCK:]
            if blk == 0:
                use_prev = col_k > row_q + no_prev_shift
                s = jnp.where(use_prev, s_prev, jnp.where(upper, NEG_INF, s_cur))
            else:
                s = jnp.where(upper, s_prev, s_cur)
            sink = jnp.concatenate(
                [jnp.full((BLOCK, LANES), sink_ref[h * GROUP + g], jnp.float32)
                 for g in range(GROUP)], axis=0)
            m = jnp.maximum(jnp.max(s, axis=-1, keepdims=True), sink)
            p = jnp.exp(s - m).astype(bf)
            e_sink = jnp.exp(sink - m)
            p_band = jnp.concatenate(
                [jnp.where(upper, p, zero_bf), jnp.where(upper, zero_bf, p)], axis=1)
            rhs = jnp.concatenate([
                jnp.concatenate([jnp.where(lo2, v_band, zero_bf), ones_lo], axis=1),
                jnp.concatenate([jnp.where(lo2, zero_bf, v_band), ones_hi], axis=1)], axis=0)
            for pair in range(GROUP // 2):
                r0 = 2 * pair * BLOCK
                lhs = jnp.concatenate(
                    [p_band[r0:r0 + BLOCK], p_band[r0 + BLOCK:r0 + 2 * BLOCK]], axis=1)
                o = _bdot(lhs, rhs)
                es = jnp.where(lo, e_sink[r0:r0 + BLOCK], e_sink[r0 + BLOCK:r0 + 2 * BLOCK])
                out = o[:, :LANES] / (o[:, LANES:] + es)
                c0 = h * MXU_DIM + pair * LANES
                attn_ref[rows, c0:c0 + LANES] = out.astype(bf)

    y = _bdot(attn_ref[...], wo_ref[...])
    o_ref[...] = x_ref[...] + mod_ref[2:3, :] * y


def _attn_call(sinks, x2, mod0, q, kd, vd, wo_bf, seq):
    n_tok = x2.shape[0]
    tps = seq // ATTN_TQ
    per = ATTN_TQ // BLOCK
    tok = lambda i: (i, 0)
    cur = lambda i: (0, i, 0)
    prv = lambda i: (0, jnp.maximum(i * per - 1, 0), 0)
    return pl.pallas_call(
        functools.partial(_attn_kernel, tiles_per_seq=tps),
        grid=(n_tok // ATTN_TQ,),
        in_specs=[
            pl.BlockSpec(memory_space=pltpu.SMEM),
            pl.BlockSpec((ATTN_TQ, D_MODEL), tok),
            pl.BlockSpec((None, 6, D_MODEL), lambda i: (i // tps, 0, 0)),
            pl.BlockSpec((ATTN_TQ, Q_DIM), tok),
            pl.BlockSpec((N_KV_HEADS, ATTN_TQ, LANES), cur),
            pl.BlockSpec((N_KV_HEADS, BLOCK, LANES), prv),
            pl.BlockSpec((N_KV_HEADS, ATTN_TQ, LANES), cur),
            pl.BlockSpec((N_KV_HEADS, BLOCK, LANES), prv),
            pl.BlockSpec((Q_DIM, D_MODEL), lambda i: (0, 0)),
        ],
        out_specs=pl.BlockSpec((ATTN_TQ, D_MODEL), tok),
        out_shape=jax.ShapeDtypeStruct((n_tok, D_MODEL), jnp.float32),
        scratch_shapes=[pltpu.VMEM((ATTN_TQ, Q_DIM), jnp.bfloat16)],
        compiler_params=pltpu.CompilerParams(
            dimension_semantics=("arbitrary",), vmem_limit_bytes=VMEM_LIMIT_BYTES),
        name="swa_attn_out",
    )(sinks, x2, mod0, q, kd, kd, vd, vd, wo_bf)


def _pool_mix(x, t_in_seq, mod_ref, gain_ref, pw_ref, ps_ref, carry_ref):
    h = _adaln(x, gain_ref[...], mod_ref[1:2, :], mod_ref[0:1, :])
    tm = x.shape[0]
    pos1 = (t_in_seq * tm + 1 + lax.broadcasted_iota(jnp.int32, (tm, 1), 0)).astype(jnp.float32)
    ys = []
    for g, w in enumerate(POOL_WINDOWS):
        cols = slice(g * POOL_GROUP_DIM, (g + 1) * POOL_GROUP_DIM)
        hg = h[:, cols]
        ext = jnp.concatenate([carry_ref[:, cols], hg], axis=0)
        acc = ext
        step = 1
        while step < w:
            acc = acc + pltpu.roll(acc, step, 0)
            step *= 2
        win = acc[MAX_POOL_WINDOW:]
        d = win / jnp.minimum(pos1, float(w)) - hg
        ys.append(_bdot(d.astype(jnp.bfloat16), pw_ref[g]))
    carry_ref[...] = h[tm - MAX_POOL_WINDOW:]
    y = jnp.concatenate(ys, axis=1) * ps_ref[...]
    return x + mod_ref[2:3, :] * y


def _causal_conv(u, carry8, w, b):
    w0, w1, w2 = w[0:1, :], w[1:2, :], w[2:3, :]
    full = b + w2 * u + w1 * pltpu.roll(u, 1, 0) + w0 * pltpu.roll(u, 2, 0)
    uh = u[:CONV_HEAD]
    e = jnp.concatenate([carry8, uh], axis=0)
    e1 = pltpu.roll(e, 1, 0)[CARRY_ROWS:]
    e2 = pltpu.roll(e, 2, 0)[CARRY_ROWS:]
    head = b + w2 * uh + w1 * e1 + w0 * e2
    return full, head


def _ffn_kernel(*refs, tiles_per_seq, pool_first):
    if pool_first:
        (x_ref, mod_ref, gain_ref, wup_ref, cw_ref, cb_ref, wdown_ref, pgain_ref, pw_ref, ps_ref,
         o_ref, carry_ref, act_ref, pcarry_ref) = refs
    else:
        x_ref, mod_ref, gain_ref, wup_ref, cw_ref, cb_ref, wdown_ref, o_ref, carry_ref, act_ref = refs
    i = pl.program_id(0)
    t_in_seq = i % tiles_per_seq

    @pl.when(t_in_seq == 0)
    def _():
        carry_ref[...] = jnp.zeros_like(carry_ref)
        if pool_first:
            pcarry_ref[...] = jnp.zeros_like(pcarry_ref)

    x = x_ref[...]
    if pool_first:
        x = _pool_mix(x, t_in_seq, mod_ref, pgain_ref, pw_ref, ps_ref, pcarry_ref)
    tm = x.shape[0]
    h = _adaln(x, gain_ref[...], mod_ref[4:5, :], mod_ref[3:4, :]).astype(jnp.bfloat16)
    for c in range(D_FF // FFN_FC):
        parts = []
        for base in (0, D_FF):
            cols = slice(base + c * FFN_FC, base + (c + 1) * FFN_FC)
            u = _bdot(h, wup_ref[:, cols])
            carry8 = carry_ref[:, cols]
            carry_ref[:, cols] = u[tm - CARRY_ROWS:]
            parts.append(_causal_conv(u, carry8, cw_ref[:, cols], cb_ref[:, cols]))
        (g_full, g_head), (v_full, v_head) = parts
        acols = slice(c * FFN_FC, (c + 1) * FFN_FC)
        act_ref[:, acols] = (g_full * jax.nn.sigmoid(g_full) * v_full).astype(jnp.bfloat16)
        act_ref[0:CONV_HEAD, acols] = (g_head * jax.nn.sigmoid(g_head) * v_head).astype(jnp.bfloat16)
    y = _bdot(act_ref[...], wdown_ref[...])
    o_ref[...] = x + mod_ref[5:6, :] * y


def _ffn_call(layer, x2, mod, gain, wup_bf, conv_w, conv_b, wdown_bf, seq, pool=None):
    n_tok = x2.shape[0]
    tps = seq // FFN_TM
    tok = lambda i: (i, 0)
    slab = lambda i: (layer, 0, 0)
    whole2 = lambda i: (0, 0)
    in_specs = [
        pl.BlockSpec((FFN_TM, D_MODEL), tok),
        pl.BlockSpec((None, None, 6, D_MODEL), lambda i: (layer, i // tps, 0, 0)),
        pl.BlockSpec((None, 1, D_MODEL), slab),
        pl.BlockSpec((None, D_MODEL, 2 * D_FF), slab, pipeline_mode=pl.Buffered(1)),
        pl.BlockSpec((None, 3, 2 * D_FF), slab),
        pl.BlockSpec((None, 1, 2 * D_FF), slab),
        pl.BlockSpec((None, D_FF, D_MODEL), slab, pipeline_mode=pl.Buffered(1)),
    ]
    scratch = [
        pltpu.VMEM((CARRY_ROWS, 2 * D_FF), jnp.float32),
        pltpu.VMEM((FFN_TM, D_FF), jnp.bfloat16),
    ]
    args = [x2, mod, gain, wup_bf, conv_w, conv_b, wdown_bf]
    if pool is not None:
        ng = len(POOL_WINDOWS)
        in_specs += [
            pl.BlockSpec((1, D_MODEL), whole2),
            pl.BlockSpec((ng, POOL_GROUP_DIM, POOL_GROUP_DIM), lambda i: (0, 0, 0)),
            pl.BlockSpec((1, D_MODEL), whole2),
        ]
        scratch.append(pltpu.VMEM((MAX_POOL_WINDOW, D_MODEL), jnp.float32))
        args += list(pool)
    return pl.pallas_call(
        functools.partial(_ffn_kernel, tiles_per_seq=tps, pool_first=pool is not None),
        grid=(n_tok // FFN_TM,),
        in_specs=in_specs,
        out_specs=pl.BlockSpec((FFN_TM, D_MODEL), tok),
        out_shape=jax.ShapeDtypeStruct((n_tok, D_MODEL), jnp.float32),
        scratch_shapes=scratch,
        compiler_params=pltpu.CompilerParams(
            dimension_semantics=("arbitrary",), vmem_limit_bytes=VMEM_LIMIT_BYTES),
        name="pool_conv_glu_ffn" if pool is not None else "conv_glu_ffn",
    )(*args)


def _rope_lane_tables(seq):
    half = ROPE_DIM // 2
    inv_freq = ROPE_THETA ** (-jnp.arange(0, half, dtype=jnp.float32) * 2.0 / ROPE_DIM)
    inv_lane = jnp.tile(jnp.pad(jnp.tile(inv_freq, 2), (0, HEAD_DIM - ROPE_DIM)), LANES // HEAD_DIM)
    ang_r = jnp.arange(QKV_TM, dtype=jnp.float32)[:, None] * inv_lane[None, :]
    ang_b = jnp.arange(0, seq, QKV_TM, dtype=jnp.float32)[:, None] * inv_lane[None, :]
    rot = jnp.stack([jnp.cos(ang_r), jnp.sin(ang_r)])
    base = jnp.stack([jnp.cos(ang_b), jnp.sin(ang_b)], axis=1)
    return rot, base


def kernel(x, c, mod_w, mod_b, mix_norm_gain, ffn_norm_gain, w_qkv, q_gain, k_gain, sinks, w_o,
           pool_w, pool_scale, w_up, conv_w, conv_b, w_down):
    batch, seq, d = x.shape
    assert d == D_MODEL and seq % max(QKV_TM, ATTN_TQ, FFN_TM) == 0
    x2 = x.reshape(batch * seq, d)

    c_pad = jnp.pad(c, ((0, 8 - batch), (0, 0)))
    mod = _modulation(c_pad, mod_w, mod_b)[:, :batch]
    mod = mod.reshape(DEPTH, batch, 6, d)

    rope_rot, rope_base = _rope_lane_tables(seq)
    scale = 1.0 / math.sqrt(HEAD_DIM)
    qk_gain = jnp.concatenate(
        [jnp.tile(q_gain[0] * scale, N_Q_HEADS), jnp.tile(k_gain[0], N_KV_HEADS)])[None, :]

    later = (w_o[0], pool_w[0].reshape(-1, POOL_GROUP_DIM), w_up.reshape(-1, 2 * D_FF),
             w_down.reshape(-1, d))
    (q, kd, vd), (wo_bf, pw_bf, wup_bf, wdown_bf) = _qkv_call(
        x2, mod[0], mix_norm_gain[0:1], w_qkv[0], qk_gain, rope_rot, rope_base, later, seq)
    x2 = _attn_call(sinks[0], x2, mod[0], q, kd, vd, wo_bf, seq)
    ffn_args = (mod, ffn_norm_gain[:, None, :], wup_bf.reshape(w_up.shape), conv_w,
                conv_b[:, None, :], wdown_bf.reshape(w_down.shape), seq)
    x2 = _ffn_call(0, x2, *ffn_args)
    x2 = _ffn_call(1, x2, *ffn_args,
                   pool=(mix_norm_gain[1:2], pw_bf.reshape(pool_w.shape[1:]), pool_scale[0:1]))
    return x2.reshape(batch, seq, d)
```
